```python
import jax, jax.numpy as jnp
from jax import lax
import numpy as np

D_MODEL = 1024
BATCH = 8
SEQ = 8192
DEPTH = 1
DEC_BATCH = 128
DEC_SEQ = 8
PAST_LEN = 8192
PAGE_SIZE = 128

N_META = 16
SB_HEADS = 8
SB_HEAD_DIM = 64
SB_WIDTH = SB_HEADS * SB_HEAD_DIM
SB_BLOCK = 128
SB_BIAS_INIT = -8.0
GLA_HEADS = 4
GLA_DK = 64
GLA_DV = 128
GLA_KEY_WIDTH = GLA_HEADS * GLA_DK
GLA_WIDTH = GLA_HEADS * GLA_DV
GLA_GATE_RANK = 16
GLA_TAU = 16.0
GLA_CHUNK = 64
D_FF = 2816
ALPHA = (2.0 * DEPTH) ** 0.25
BETA = (8.0 * DEPTH) ** -0.25
LN_EPS = 1e-5
RMS_EPS = 1e-6
MIX_SIZES = (SB_WIDTH, SB_WIDTH, SB_WIDTH, GLA_KEY_WIDTH, GLA_KEY_WIDTH, GLA_WIDTH, GLA_WIDTH, GLA_GATE_RANK)
D_MIX_IN = 3 * SB_WIDTH + 2 * GLA_KEY_WIDTH + 2 * GLA_WIDTH + GLA_GATE_RANK

kernel_name = "hymba_stickbreak_gla_macaron_deepnorm_step"


def layer_norm(x, g, b):
    xf = x.astype(jnp.float32)
    mu = jnp.mean(xf, axis=-1, keepdims=True)
    var = jnp.mean(jnp.square(xf - mu), axis=-1, keepdims=True)
    return ((xf - mu) * lax.rsqrt(var + LN_EPS) * g + b).astype(x.dtype)


def post_norm(x, sub, g, b):
    return layer_norm(ALPHA * x + sub, g, b)


def swiglu(x, w_in, w_out):
    gate, up = jnp.split(x @ w_in, 2, axis=-1)
    return (jax.nn.silu(gate) * up) @ w_out


def mix_projections(h, w_mix, w_gate_up, b_gate):
    B, T = h.shape[0], h.shape[1]
    f32 = jnp.float32
    p = h @ w_mix
    sq, sk, sv, gq, gk, gv, gr, glr = jnp.split(p, np.cumsum(MIX_SIZES)[:-1].tolist(), axis=-1)
    log_gate = jax.nn.log_sigmoid((glr @ w_gate_up + b_gate).astype(f32)) / GLA_TAU
    sb = lambda a: a.reshape(B, T, SB_HEADS, SB_HEAD_DIM)
    gk_ = lambda a: a.reshape(B, T, GLA_HEADS, GLA_DK)
    return (sb(sq), sb(sk), sb(sv),
            gk_(gq).astype(f32) * (GLA_DK ** -0.5), gk_(gk).astype(f32),
            gv.reshape(B, T, GLA_HEADS, GLA_DV).astype(f32), gr, gk_(log_gate))


def stick_breaking_weights(z, mask):
    log_beta = jax.nn.log_sigmoid(z)
    log_keep = jnp.where(mask, jax.nn.log_sigmoid(-z), 0.0)
    shifted = jnp.concatenate([log_keep[..., 1:], jnp.zeros_like(log_keep[..., :1])], axis=-1)
    log_stick = lax.cumsum(shifted, axis=z.ndim - 1, reverse=True)
    return jnp.where(mask, jnp.exp(log_beta + log_stick), 0.0)


def sb_attend(qb, q0, keys, vals, kpos, bias):
    qpos = q0 + jnp.arange(qb.shape[1])
    z = (jnp.einsum('bqhd,bkhd->bhqk', qb, keys).astype(jnp.float32) * (SB_HEAD_DIM ** -0.5)
         + bias.astype(jnp.float32)[:, None, None])
    w = stick_breaking_weights(z, kpos[None, :] < qpos[:, None])
    return jnp.einsum('bhqk,bkhd->bqhd', w.astype(vals.dtype), vals)


def sb_prompt(q, k, v, bias):
    B, L = q.shape[0], q.shape[1]
    kpos = jnp.arange(L)
    o_meta = sb_attend(q[:, :N_META], 0, k[:, :N_META], v[:, :N_META], kpos[:N_META], bias)
    n_blk = (L - N_META) // SB_BLOCK
    qb = q[:, N_META:].reshape(B, n_blk, SB_BLOCK, SB_HEADS, SB_HEAD_DIM).swapaxes(0, 1)
    starts = N_META + SB_BLOCK * jnp.arange(n_blk)
    o_blk = lax.map(lambda a: sb_attend(a[0], a[1], k, v, kpos, bias), (qb, starts))
    o_real = o_blk.swapaxes(0, 1).reshape(B, L - N_META, SB_HEADS, SB_HEAD_DIM)
    return jnp.concatenate([o_meta, o_real], axis=1)


def sb_sample(q, k_new, v_new, k_past, v_past, bias):
    P, T = k_past.shape[1], q.shape[1]
    z = jnp.concatenate([jnp.einsum('bqhd,bkhd->bhqk', q, k_past),
                         jnp.einsum('bqhd,bkhd->bhqk', q, k_new)], axis=-1)
    z = z.astype(jnp.float32) * (SB_HEAD_DIM ** -0.5) + bias.astype(jnp.float32)[:, None, None]
    qpos = P + jnp.arange(T)
    kpos = jnp.arange(P + T)
    w = stick_breaking_weights(z, kpos[None, :] < qpos[:, None]).astype(v_new.dtype)
    return (jnp.einsum('bhqk,bkhd->bqhd', w[..., :P], v_past)
            + jnp.einsum('bhqk,bkhd->bqhd', w[..., P:], v_new))


def gla_chunk(S, q, k, v, lg):
    C = q.shape[1]
    b = jnp.cumsum(lg, axis=1)
    o_inter = jnp.einsum('bthk,bhkv->bthv', q * jnp.exp(b), S)
    mask = jnp.tril(jnp.ones((C, C), dtype=bool))[None, :, :, None, None]
    diff = b[:, :, None] - b[:, None, :]
    decay = jnp.exp(jnp.where(mask, diff, -jnp.inf))
    att = jnp.einsum('bthk,bshk,btshk->bhts', q, k, decay)
    o_intra = jnp.einsum('bhts,bshv->bthv', att, v)
    b_last = b[:, -1]
    S_new = (jnp.exp(b_last)[..., None] * S
             + jnp.einsum('bshk,bshv->bhkv', k * jnp.exp(b_last[:, None] - b), v))
    return S_new, o_inter + o_intra


def gla_prompt(q, k, v, lg):
    B, L = q.shape[0], q.shape[1]
    S0 = jnp.zeros((B, GLA_HEADS, GLA_DK, GLA_DV), jnp.float32)
    S, o_meta = gla_chunk(S0, q[:, :N_META], k[:, :N_META], v[:, :N_META], lg[:, :N_META])
    n_c = (L - N_META) // GLA_CHUNK
    to_chunks = lambda a: a[:, N_META:].reshape(B, n_c, GLA_CHUNK, *a.shape[2:]).swapaxes(0, 1)
    def step(state, xs):
        return gla_chunk(state, *xs)
    S, o_c = lax.scan(step, S, (to_chunks(q), to_chunks(k), to_chunks(v), to_chunks(lg)))
    o_real = o_c.swapaxes(0, 1).reshape(B, L - N_META, GLA_HEADS, GLA_DV)
    return jnp.concatenate([o_meta, o_real], axis=1), S


def mix_output(sb_o, gla_o, r, g_norm, w_o):
    B, T = r.shape[0], r.shape[1]
    dt = w_o.dtype
    gla_n = gla_o * lax.rsqrt(jnp.mean(jnp.square(gla_o), axis=-1, keepdims=True) + RMS_EPS) * g_norm
    gla_y = gla_n.reshape(B, T, GLA_WIDTH) * jax.nn.silu(r.astype(jnp.float32))
    cat = jnp.concatenate([sb_o.reshape(B, T, SB_WIDTH).astype(dt), gla_y.astype(dt)], axis=-1)
    return cat @ w_o


def setup_inputs(seed: int = 0) -> dict:
    key = jax.random.key(seed)
    ks = jax.random.split(key, 20)
    nrm = jax.random.normal
    n_pages = PAST_LEN // PAGE_SIZE
    n_used = DEC_BATCH * n_pages
    n_phys = n_used + n_used // 4
    x_prompt = nrm(ks[0], (BATCH, SEQ, D_MODEL), jnp.float32)
    x_sample = nrm(ks[1], (DEC_BATCH, DEC_SEQ, D_MODEL), jnp.float32)
    cache_k = nrm(ks[2], (DEPTH, n_phys, PAGE_SIZE, SB_HEADS, SB_HEAD_DIM), jnp.float32)
    cache_v = nrm(ks[3], (DEPTH, n_phys, PAGE_SIZE, SB_HEADS, SB_HEAD_DIM), jnp.float32) * BETA
    state_gla = nrm(ks[4], (DEPTH, DEC_BATCH, GLA_HEADS, GLA_DK, GLA_DV), jnp.float32) * 0.5
    page_table = jax.random.permutation(ks[5], n_phys)[:n_used].reshape(DEC_BATCH, n_pages).astype(jnp.int32)
    meta_tokens = nrm(ks[6], (N_META, D_MODEL), jnp.float32)
    ln_g = 1.0 + 0.02 * nrm(ks[7], (DEPTH, 3, D_MODEL), jnp.float32)
    ln_b = 0.02 * nrm(ks[8], (DEPTH, 3, D_MODEL), jnp.float32)
    w_ffn1_in = nrm(ks[9], (DEPTH, D_MODEL, 2 * D_FF), jnp.float32) * D_MODEL ** -0.5
    w_ffn1_out = nrm(ks[10], (DEPTH, D_FF, D_MODEL), jnp.float32) * (D_FF ** -0.5 * BETA)
    col_scale = jnp.concatenate([jnp.full((n,), BETA if i in (2, 5) else 1.0, jnp.float32)
                                 for i, n in enumerate(MIX_SIZES)])
    w_mix_in = nrm(ks[11], (DEPTH, D_MODEL, D_MIX_IN), jnp.float32) * D_MODEL ** -0.5 * col_scale
    w_gate_up = nrm(ks[12], (DEPTH, GLA_GATE_RANK, GLA_KEY_WIDTH), jnp.float32) * GLA_GATE_RANK ** -0.5
    b_gate = 0.1 * nrm(ks[13], (DEPTH, GLA_KEY_WIDTH), jnp.float32)
    b_sb = SB_BIAS_INIT + 0.5 * nrm(ks[18], (DEPTH, SB_HEADS), jnp.float32)
    g_gla_norm = 1.0 + 0.02 * nrm(ks[14], (DEPTH, GLA_DV), jnp.float32)
    w_out = nrm(ks[15], (DEPTH, D_MODEL, D_MODEL), jnp.float32) * (D_MODEL ** -0.5 * BETA)
    w_ffn2_in = nrm(ks[16], (DEPTH, D_MODEL, 2 * D_FF), jnp.float32) * D_MODEL ** -0.5
    w_ffn2_out = nrm(ks[17], (DEPTH, D_FF, D_MODEL), jnp.float32) * (D_FF ** -0.5 * BETA)
    return {"x_prompt": x_prompt, "x_sample": x_sample, "cache_k": cache_k, "cache_v": cache_v,
            "state_gla": state_gla, "page_table": page_table, "meta_tokens": meta_tokens,
            "ln_g": ln_g, "ln_b": ln_b, "w_ffn1_in": w_ffn1_in, "w_ffn1_out": w_ffn1_out,
            "w_mix_in": w_mix_in, "w_gate_up": w_gate_up, "b_gate": b_gate, "b_sb": b_sb,
            "g_gla_norm": g_gla_norm, "w_out": w_out, "w_ffn2_in": w_ffn2_in, "w_ffn2_out": w_ffn2_out}


def reference(x_prompt, x_sample, cache_k, cache_v, state_gla, page_table, meta_tokens,
              ln_g, ln_b, w_ffn1_in, w_ffn1_out, w_mix_in, w_gate_up, b_gate, b_sb,
              g_gla_norm, w_out, w_ffn2_in, w_ffn2_out):
    B = x_prompt.shape[0]
    DB = x_sample.shape[0]
    n_pages = page_table.shape[1]
    meta = jnp.broadcast_to(meta_tokens.astype(x_prompt.dtype)[None], (B, N_META, D_MODEL))
    hp = jnp.concatenate([meta, x_prompt], axis=1)
    hs = x_sample
    kp_rows, vp_rows, sp_list, ks_rows, vs_rows, ss_list = [], [], [], [], [], []
    for l in range(DEPTH):
        hp = post_norm(hp, 0.5 * swiglu(hp, w_ffn1_in[l], w_ffn1_out[l]), ln_g[l, 0], ln_b[l, 0])
        hs = post_norm(hs, 0.5 * swiglu(hs, w_ffn1_in[l], w_ffn1_out[l]), ln_g[l, 0], ln_b[l, 0])
        sq, sk, sv, gq, gk, gv, gr, lg = mix_projections(hp, w_mix_in[l], w_gate_up[l], b_gate[l])
        sb_o = sb_prompt(sq, sk, sv, b_sb[l])
        gla_o, s_p = gla_prompt(gq, gk, gv, lg)
        hp = post_norm(hp, mix_output(sb_o, gla_o, gr, g_gla_norm[l], w_out[l]), ln_g[l, 1], ln_b[l, 1])
        kp_rows.append(sk)
        vp_rows.append(sv)
        sp_list.append(s_p.astype(state_gla.dtype))
        sq, sk, sv, gq, gk, gv, gr, lg = mix_projections(hs, w_mix_in[l], w_gate_up[l], b_gate[l])
        k_past = cache_k[l][page_table].reshape(DB, n_pages * PAGE_SIZE, SB_HEADS, SB_HEAD_DIM)
        v_past = cache_v[l][page_table].reshape(DB, n_pages * PAGE_SIZE, SB_HEADS, SB_HEAD_DIM)
        sb_o = sb_sample(sq, sk, sv, k_past, v_past, b_sb[l])
        s_s, gla_o = gla_chunk(state_gla[l].astype(jnp.float32), gq, gk, gv, lg)
        hs = post_norm(hs, mix_output(sb_o, gla_o, gr, g_gla_norm[l], w_out[l]), ln_g[l, 1], ln_b[l, 1])
        ks_rows.append(sk)
        vs_rows.append(sv)
        ss_list.append(s_s.astype(state_gla.dtype))
        hp = post_norm(hp, 0.5 * swiglu(hp, w_ffn2_in[l], w_ffn2_out[l]), ln_g[l, 2], ln_b[l, 2])
        hs = post_norm(hs, 0.5 * swiglu(hs, w_ffn2_in[l], w_ffn2_out[l]), ln_g[l, 2], ln_b[l, 2])
    y_prompt = hp[:, N_META:]
    y_sample = hs
    k_prompt = jnp.stack(kp_rows)
    v_prompt = jnp.stack(vp_rows)
    s_prompt = jnp.stack(sp_list)
    k_sample = jnp.stack(ks_rows)
    v_sample = jnp.stack(vs_rows)
    s_sample = jnp.stack(ss_list)
    return (y_prompt, y_sample, k_prompt, v_prompt, s_prompt, k_sample, v_sample, s_sample)
```

```python
import functools

import jax
import jax.numpy as jnp
from jax import lax
from jax.experimental import pallas as pl
from jax.experimental.pallas import tpu as pltpu

F32 = jnp.float32
BF16 = jnp.bfloat16

N_META = 16
LN_EPS = 1e-5
RMS_EPS = 1e-6
GLA_TAU = 16.0
GLA_CHUNK = 64
GLA_SUB = 16
LANES = 128
KEY_BLOCK = 256
VMEM_LIMIT_BYTES = 56 * 1024 * 1024


def _dot(a, b):
    return jnp.dot(a, b, preferred_element_type=F32)


def _dot_nt(a, b):
    return lax.dot_general(a, b, (((1,), (1,)), ((), ())), preferred_element_type=F32)


def _dot_tn(a, b):
    return lax.dot_general(a, b, (((0,), (0,)), ((), ())), preferred_element_type=F32)


def _params(n_grid):
    return pltpu.CompilerParams(dimension_semantics=("arbitrary",) * n_grid,
                                vmem_limit_bytes=VMEM_LIMIT_BYTES)


def _resident(shape):
    return pl.BlockSpec(shape, lambda *_: (0,) * len(shape), pipeline_mode=pl.Buffered(1))


def _layer_norm(y, g, b):
    mu = jnp.mean(y, axis=-1, keepdims=True)
    d = y - mu
    var = jnp.mean(d * d, axis=-1, keepdims=True)
    return d * lax.rsqrt(var + LN_EPS) * g + b


def _log_sigmoid_pair(z):
    l = jnp.log(1.0 + jnp.exp(-jnp.abs(z)))
    lb = jnp.minimum(z, 0.0) - l
    return lb, lb - z


def _ffn_ln_kernel(x_ref, wg_ref, wu_ref, wo_ref, g_ref, b_ref, o_ref, acc_ref, *, alpha):
    x = x_ref[...]
    xb = x.astype(BF16)
    for c in range(wg_ref.shape[0]):
        gate = _dot(xb, wg_ref[c])
        up = _dot(xb, wu_ref[c])
        act = (gate * jax.nn.sigmoid(gate) * up).astype(BF16)
        part = _dot(act, wo_ref[c])
        if c == 0:
            acc_ref[...] = part
        else:
            acc_ref[...] += part
    o_ref[...] = _layer_norm(alpha * x + 0.5 * acc_ref[...], g_ref[...], b_ref[...])


def _prep_ffn(w_in, w_out, tf=256):
    d, f = w_in.shape[0], w_out.shape[0]
    nc = f // tf
    assert nc * tf == f
    wg = w_in[:, :f].reshape(d, nc, tf).transpose(1, 0, 2).astype(BF16)
    wu = w_in[:, f:].reshape(d, nc, tf).transpose(1, 0, 2).astype(BF16)
    wo = w_out.reshape(nc, tf, d).astype(BF16)
    return wg, wu, wo


def _ffn_ln(x, ffn_w, g, b, alpha, tm=512):
    wg, wu, wo = ffn_w
    m, d = x.shape
    tm = min(tm, m)
    assert m % tm == 0
    row = pl.BlockSpec((tm, d), lambda i: (i, 0))
    return pl.pallas_call(
        functools.partial(_ffn_ln_kernel, alpha=alpha),
        grid=(m // tm,),
        in_specs=[row, _resident(wg.shape), _resident(wu.shape), _resident(wo.shape),
                  _resident((1, d)), _resident((1, d))],
        out_specs=row,
        out_shape=jax.ShapeDtypeStruct((m, d), F32),
        scratch_shapes=[pltpu.VMEM((tm, d), F32)],
        compiler_params=_params(1),
        name="ffn_ln",
    )(x, wg, wu, wo, g.reshape(1, d), b.reshape(1, d))


def _mix_kernel(x_ref, wsb_ref, wgl_ref, wlr_ref, wgu_ref, bg_ref,
                q_ref, kf_ref, vf_ref, kb_ref, vb_ref, gq_ref, gk_ref, gv_ref, gr_ref, lg_ref,
                *, sb_scale, gla_scale):
    xb = x_ref[...].astype(BF16)
    w = q_ref.shape[1]
    kw = gq_ref.shape[1]
    gw = gv_ref.shape[1]
    q_ref[...] = (_dot(xb, wsb_ref[:, 0:w]) * sb_scale).astype(BF16)
    k = _dot(xb, wsb_ref[:, w:2 * w])
    kf_ref[...] = k
    kb_ref[...] = k.astype(BF16)
    v = _dot(xb, wsb_ref[:, 2 * w:3 * w])
    vf_ref[...] = v
    vb_ref[...] = v.astype(BF16)
    gq_ref[...] = _dot(xb, wgl_ref[:, 0:kw]) * gla_scale
    gk_ref[...] = _dot(xb, wgl_ref[:, kw:2 * kw])
    gv_ref[...] = _dot(xb, wgl_ref[:, 2 * kw:2 * kw + gw]).astype(BF16)
    gr_ref[...] = _dot(xb, wgl_ref[:, 2 * kw + gw:2 * kw + 2 * gw])
    low_rank = _dot(xb, wlr_ref[...]).astype(BF16)
    gate_logit = _dot(low_rank, wgu_ref[...]) + bg_ref[...]
    lg_ref[...] = _log_sigmoid_pair(gate_logit)[0] * (1.0 / GLA_TAU)


def _prep_mix(w_mix, w_gate_up, b_gate, w, kw, gw):
    d = w_mix.shape[0]
    rank = w_gate_up.shape[0]
    assert w_mix.shape[1] == 3 * w + 2 * kw + 2 * gw + rank and rank <= LANES
    wsb = w_mix[:, :3 * w].astype(BF16)
    wgl = w_mix[:, 3 * w:3 * w + 2 * kw + 2 * gw].astype(BF16)
    wlr = jnp.pad(w_mix[:, 3 * w + 2 * kw + 2 * gw:], ((0, 0), (0, LANES - rank))).astype(BF16)
    wgu = jnp.pad(w_gate_up, ((0, LANES - rank), (0, 0))).astype(BF16)
    return wsb, wgl, wlr, wgu, b_gate.reshape(1, kw).astype(F32)


def _mix(x, mix_w, w, kw, gw, sb_scale, gla_scale, tm=512):
    wsb, wgl, wlr, wgu, bg = mix_w
    m, d = x.shape
    tm = min(tm, m)
    assert m % tm == 0
    spec = lambda n: pl.BlockSpec((tm, n), lambda i: (i, 0))
    sds = lambda n, dt: jax.ShapeDtypeStruct((m, n), dt)
    return pl.pallas_call(
        functools.partial(_mix_kernel, sb_scale=sb_scale, gla_scale=gla_scale),
        grid=(m // tm,),
        in_specs=[spec(d), _resident(wsb.shape), _resident(wgl.shape), _resident(wlr.shape),
                  _resident(wgu.shape), _resident(bg.shape)],
        out_specs=[spec(w), spec(w), spec(w), spec(w), spec(w),
                   spec(kw), spec(kw), spec(gw), spec(gw), spec(kw)],
        out_shape=[sds(w, BF16), sds(w, F32), sds(w, F32), sds(w, BF16), sds(w, BF16),
                   sds(kw, F32), sds(kw, F32), sds(gw, BF16), sds(gw, F32), sds(kw, F32)],
        compiler_params=_params(1),
        name="mix_proj",
    )(x, wsb, wgl, wlr, wgu, bg)


def _sb_block(z, vb, u, mask, c, acc_ref, first):
    lb, lk = _log_sigmoid_pair(z)
    if mask is not None:
        lk = jnp.where(mask, lk, 0.0)
    n = z.shape[1]
    s = _dot(lk.astype(BF16), u[:n, :n])
    w = jnp.exp(lb + s + c)
    if mask is not None:
        w = jnp.where(mask, w, 0.0)
    pv = _dot(w.astype(BF16), vb)
    if first:
        acc_ref[...] = pv
    else:
        acc_ref[...] += pv
    return c + s[:, 0:1] + lk[:, 0:1]


def _sb_prompt_kernel(bias_ref, q_ref, k_ref, v_ref, km_ref, vm_ref, u_ref, o_ref, acc_ref,
                      *, tq, n_meta, dh):
    hp = pl.program_id(1)
    i = pl.program_id(2)
    q = q_ref[0]
    lane = lax.broadcasted_iota(jnp.int32, q.shape, 1)
    zero = jnp.zeros_like(q)
    qs = jnp.concatenate([jnp.where(lane < dh, q, zero), jnp.where(lane >= dh, q, zero)], axis=0)
    row = lax.broadcasted_iota(jnp.int32, (2 * tq, 1), 0)
    bias = jnp.where(row < tq, bias_ref[2 * hp], bias_ref[2 * hp + 1])
    t_in_tile = jnp.where(row < tq, row, row - tq)
    u = u_ref[...]

    def keys(j):
        off = pl.multiple_of(j * tq, tq)
        return k_ref[0, pl.ds(off, tq), :], v_ref[0, pl.ds(off, tq), :]

    kb, vb = keys(i)
    col = lax.broadcasted_iota(jnp.int32, (2 * tq, tq), 1)
    c = _sb_block(_dot_nt(qs, kb) + bias, vb, u, col < t_in_tile,
                  jnp.zeros((2 * tq, 1), F32), acc_ref, True)

    def body(n, c):
        kb, vb = keys(i - 1 - n)
        return _sb_block(_dot_nt(qs, kb) + bias, vb, u, None, c, acc_ref, False)

    c = lax.fori_loop(0, i, body, c)

    colm = lax.broadcasted_iota(jnp.int32, (2 * tq, km_ref.shape[0]), 1)
    _sb_block(_dot_nt(qs, km_ref[...]) + bias, vm_ref[...], u, colm < n_meta, c, acc_ref, False)

    acc = acc_ref[...]
    o_ref[0] = jnp.where(lane < dh, acc[:tq], acc[tq:]).astype(BF16)


def _suffix_matrix(n):
    r = lax.broadcasted_iota(jnp.int32, (n, n), 0)
    c = lax.broadcasted_iota(jnp.int32, (n, n), 1)
    return (r > c).astype(BF16)


def _sb_prompt(bias, q, k, v, km, vm, dh, tq=KEY_BLOCK):
    b, t, w = q.shape
    assert t % tq == 0 and w % LANES == 0 and LANES == 2 * dh
    n_meta = km.shape[0]
    pad = ((0, LANES - n_meta), (0, 0))
    km = jnp.pad(km, pad)
    vm = jnp.pad(vm, pad)
    u = _suffix_matrix(tq)
    qspec = pl.BlockSpec((1, tq, LANES), lambda bi, hp, i: (bi, i, hp))
    kvspec = pl.BlockSpec((1, t, LANES), lambda bi, hp, i: (bi, 0, hp))
    mspec = pl.BlockSpec((LANES, LANES), lambda bi, hp, i: (0, hp))
    return pl.pallas_call(
        functools.partial(_sb_prompt_kernel, tq=tq, n_meta=n_meta, dh=dh),
        grid=(b, w // LANES, t // tq),
        in_specs=[pl.BlockSpec(memory_space=pltpu.SMEM), qspec, kvspec, kvspec, mspec, mspec,
                  _resident(u.shape)],
        out_specs=qspec,
        out_shape=jax.ShapeDtypeStruct((b, t, w), BF16),
        scratch_shapes=[pltpu.VMEM((2 * tq, LANES), F32)],
        compiler_params=_params(3),
        name="sb_prompt",
    )(bias, q, k, v, km, vm, u)


def _sb_sample_kernel(pt_ref, bias_ref, q_ref, kn_ref, vn_ref, *rest, n_group, heads):
    kp = rest[:n_group]
    vp = rest[n_group:2 * n_group]
    u_ref, o_ref, qbd_ref, c_ref, acc_ref = rest[2 * n_group:]
    p = pl.program_id(1)
    t_new, w = q_ref.shape[1], q_ref.shape[2]
    dh = w // heads
    ht = heads * t_new
    page = kp[0].shape[1]
    head_of_row = lax.broadcasted_iota(jnp.int32, (ht, w), 0) // t_new
    head_of_lane = lax.broadcasted_iota(jnp.int32, (ht, w), 1) // dh
    own = head_of_row == head_of_lane
    u = u_ref[...]
    bias = bias_ref[...]

    @pl.when(p == 0)
    def _start():
        qt = jnp.concatenate([q_ref[0]] * heads, axis=0)
        qbd = jnp.where(own, qt, 0.0).astype(BF16)
        qbd_ref[...] = qbd
        fill = jnp.zeros((LANES - t_new, w), F32)
        kn = jnp.concatenate([kn_ref[0], fill], axis=0).astype(BF16)
        vn = jnp.concatenate([vn_ref[0], fill], axis=0).astype(BF16)
        col = lax.broadcasted_iota(jnp.int32, (ht, LANES), 1)
        t_of_row = lax.broadcasted_iota(jnp.int32, (ht, LANES), 0) % t_new
        c_ref[...] = _sb_block(_dot_nt(qbd, kn) + bias, vn, u, col < t_of_row,
                               jnp.zeros((ht, 1), F32), acc_ref, True)

    qbd = qbd_ref[...]
    per_block = KEY_BLOCK // page
    c = c_ref[...]
    for blk in reversed(range(n_group // per_block)):
        ids = range(blk * per_block, (blk + 1) * per_block)
        z = jnp.concatenate([_dot_nt(qbd, kp[g][0].astype(BF16)) for g in ids], axis=1)
        vb = jnp.concatenate([vp[g][0] for g in ids], axis=0).astype(BF16)
        c = _sb_block(z + bias, vb, u, None, c, acc_ref, False)
    c_ref[...] = c

    @pl.when(p == pl.num_programs(1) - 1)
    def _finish():
        acc = jnp.where(own, acc_ref[...], 0.0)
        out = acc[0:t_new]
        for h in range(1, heads):
            out = out + acc[h * t_new:(h + 1) * t_new]
        o_ref[0] = out


def _sb_sample(bias, q, k_new, v_new, cache_k, cache_v, page_table, heads):
    db, t_new, w = q.shape
    n_phys, page, _ = cache_k.shape
    n_pages = page_table.shape[1]
    assert KEY_BLOCK % page == 0 and t_new <= LANES
    per_block = KEY_BLOCK // page
    n_group = 8 if n_pages % 8 == 0 else per_block
    assert n_pages % n_group == 0 and n_group % per_block == 0
    n_steps = n_pages // n_group
    ht = heads * t_new
    bias_rows = jnp.repeat(bias.astype(F32), t_new).reshape(ht, 1)
    u = _suffix_matrix(KEY_BLOCK)
    tok = pl.BlockSpec((1, t_new, w), lambda b, p, pt: (b, 0, 0))

    def page_spec(g):
        return pl.BlockSpec((1, page, w),
                            lambda b, p, pt: (pt[b, (n_steps - 1 - p) * n_group + g], 0, 0))

    const = lambda shape: pl.BlockSpec(shape, lambda b, p, pt: (0,) * len(shape))
    grid_spec = pltpu.PrefetchScalarGridSpec(
        num_scalar_prefetch=1,
        grid=(db, n_steps),
        in_specs=[const((ht, 1)), tok, tok, tok]
                 + [page_spec(g) for g in range(n_group)] * 2 + [const(u.shape)],
        out_specs=tok,
        scratch_shapes=[pltpu.VMEM((ht, w), BF16), pltpu.VMEM((ht, 1), F32),
                        pltpu.VMEM((ht, w), F32)],
    )
    return pl.pallas_call(
        functools.partial(_sb_sample_kernel, n_group=n_group, heads=heads),
        grid_spec=grid_spec,
        out_shape=jax.ShapeDtypeStruct((db, t_new, w), F32),
        compiler_params=_params(2),
        name="sb_sample",
    )(page_table, bias_rows, q, k_new, v_new, *([cache_k] * n_group), *([cache_v] * n_group), u)


def _gla_kernel(q_ref, k_ref, lg_ref, v_ref, r_ref, s0_ref, gn_ref, ltri_ref, y_ref, sout_ref,
                st_ref, *, chunk, sub, heads):
    t = pl.program_id(1)
    tb, kw = q_ref.shape[1], q_ref.shape[2]
    gw = v_ref.shape[2]
    dk, dv = kw // heads, gw // heads
    n_sub = chunk // sub

    @pl.when(t == 0)
    def _init():
        st_ref[...] = s0_ref[0]

    own_state = (lax.broadcasted_iota(jnp.int32, (gw, kw), 0) // dv
                 == lax.broadcasted_iota(jnp.int32, (gw, kw), 1) // dk)
    lane_head = lax.broadcasted_iota(jnp.int32, (sub, kw), 1) // dk
    srow = lax.broadcasted_iota(jnp.int32, (chunk, kw), 0)
    att_col = lax.broadcasted_iota(jnp.int32, (heads * sub, chunk), 1)
    att_t = lax.broadcasted_iota(jnp.int32, (heads * sub, chunk), 0) % sub
    ltri = ltri_ref[...]
    gn = gn_ref[...]

    def chunk_body(ci, carry):
        off = pl.multiple_of(ci * chunk, chunk)
        rows = pl.ds(off, chunk)
        q = q_ref[0, rows, :]
        k = k_ref[0, rows, :]
        lg = lg_ref[0, rows, :]
        v = v_ref[0, rows, :]
        lg_hi = lg.astype(BF16)
        lg_lo = (lg - lg_hi.astype(F32)).astype(BF16)
        bc = _dot(ltri, lg_hi) + _dot(ltri, lg_lo)
        b_last = bc[chunk - 1:chunk, :]
        st = st_ref[...]
        o_inter = _dot_nt((q * jnp.exp(bc)).astype(BF16), st.astype(BF16))
        intra = []
        for sc in range(n_sub):
            lo, hi = sc * sub, (sc + 1) * sub
            ref = bc[lo - 1:lo, :] if sc > 0 else jnp.zeros((1, kw), F32)
            qd = q[lo:hi] * jnp.exp(bc[lo:hi] - ref)
            seen = srow < hi
            kd = (jnp.where(seen, k, 0.0) * jnp.exp(jnp.where(seen, ref - bc, 0.0))).astype(BF16)
            qs = jnp.concatenate([jnp.where(lane_head == h, qd, 0.0) for h in range(heads)],
                                 axis=0).astype(BF16)
            att = _dot_nt(qs, kd)
            att = jnp.where(att_col <= lo + att_t, att, 0.0)
            ov = _dot(att.astype(BF16), v)
            intra.append(jnp.concatenate(
                [ov[h * sub:(h + 1) * sub, h * dv:(h + 1) * dv] for h in range(heads)], axis=1))
        o = o_inter + jnp.concatenate(intra, axis=0)
        normed = []
        for h in range(heads):
            oh = o[:, h * dv:(h + 1) * dv]
            normed.append(oh * lax.rsqrt(jnp.mean(oh * oh, axis=-1, keepdims=True) + RMS_EPS))
        r = r_ref[0, rows, :]
        y = jnp.concatenate(normed, axis=1) * gn * (r * jax.nn.sigmoid(r))
        y_ref[0, rows, :] = y.astype(BF16)
        k_last = (k * jnp.exp(b_last - bc)).astype(BF16)
        st_ref[...] = st * jnp.exp(b_last) + jnp.where(own_state, _dot_tn(v, k_last), 0.0)
        return carry

    lax.fori_loop(0, tb // chunk, chunk_body, 0)

    @pl.when(t == pl.num_programs(1) - 1)
    def _final():
        sout_ref[0] = st_ref[...]


def _gla(q, k, lg, v, r, st0, g_norm, heads, chunk, tb):
    b, t, kw = q.shape
    gw = v.shape[2]
    sub = min(GLA_SUB, chunk)
    tb = min(tb, t)
    assert t % tb == 0 and tb % chunk == 0 and chunk % sub == 0
    shared_state = st0.shape[0] == 1
    ltri = (lax.broadcasted_iota(jnp.int32, (chunk, chunk), 0)
            >= lax.broadcasted_iota(jnp.int32, (chunk, chunk), 1)).astype(BF16)
    gn = jnp.tile(g_norm.astype(F32), heads).reshape(1, gw)
    tok = lambda n: pl.BlockSpec((1, tb, n), lambda bi, ti: (bi, ti, 0))
    state_in = pl.BlockSpec((1, gw, kw), lambda bi, ti: (0 if shared_state else bi, 0, 0))
    state_out = pl.BlockSpec((1, gw, kw), lambda bi, ti: (bi, 0, 0))
    const = lambda shape: pl.BlockSpec(shape, lambda bi, ti: (0,) * len(shape))
    return pl.pallas_call(
        functools.partial(_gla_kernel, chunk=chunk, sub=sub, heads=heads),
        grid=(b, t // tb),
        in_specs=[tok(kw), tok(kw), tok(kw), tok(gw), tok(gw), state_in, const((1, gw)),
                  const((chunk, chunk))],
        out_specs=[tok(gw), state_out],
        out_shape=[jax.ShapeDtypeStruct((b, t, gw), BF16),
                   jax.ShapeDtypeStruct((b, gw, kw), F32)],
        scratch_shapes=[pltpu.VMEM((gw, kw), F32)],
        compiler_params=_params(2),
        name="gla",
    )(q, k, lg, v, r, st0, gn, ltri)


def _state_to_blockdiag(s):
    n, h, dk, dv = s.shape
    eye = jnp.eye(h, dtype=s.dtype)
    st = s.transpose(0, 1, 3, 2)[:, :, :, None, :] * eye[None, :, None, :, None]
    return st.reshape(n, h * dv, h * dk)


def _state_from_blockdiag(st, h):
    n, gw, kw = st.shape
    dv, dk = gw // h, kw // h
    st5 = st.reshape(n, h, dv, h, dk)
    return jnp.stack([st5[:, i, :, i, :] for i in range(h)], axis=1).transpose(0, 1, 3, 2)


def _outproj_ln_kernel(h_ref, sb_ref, gl_ref, w1_ref, w2_ref, g_ref, b_ref, o_ref, *, alpha):
    sub = _dot(sb_ref[...], w1_ref[...]) + _dot(gl_ref[...], w2_ref[...])
    o_ref[...] = _layer_norm(alpha * h_ref[...] + sub, g_ref[...], b_ref[...])


def _outproj_ln(h, sb_o, gla_y, w1, w2, g, b, alpha, tm=512):
    m, d = h.shape
    tm = min(tm, m)
    assert m % tm == 0
    spec = lambda n: pl.BlockSpec((tm, n), lambda i: (i, 0))
    return pl.pallas_call(
        functools.partial(_outproj_ln_kernel, alpha=alpha),
        grid=(m // tm,),
        in_specs=[spec(d), spec(sb_o.shape[1]), spec(gla_y.shape[1]), _resident(w1.shape),
                  _resident(w2.shape), _resident((1, d)), _resident((1, d))],
        out_specs=spec(d),
        out_shape=jax.ShapeDtypeStruct((m, d), F32),
        compiler_params=_params(1),
        name="outproj_ln",
    )(h, sb_o, gla_y, w1, w2, g.reshape(1, d), b.reshape(1, d))


def kernel(x_prompt, x_sample, cache_k, cache_v, state_gla, page_table, meta_tokens, ln_g, ln_b,
           w_ffn1_in, w_ffn1_out, w_mix_in, w_gate_up, b_gate, b_sb, g_gla_norm, w_out,
           w_ffn2_in, w_ffn2_out):
    depth = w_out.shape[0]
    assert depth == 1
    alpha = (2.0 * depth) ** 0.25
    b, t, d = x_prompt.shape
    db, ts, _ = x_sample.shape
    _, n_phys, page, heads, dh = cache_k.shape
    w = heads * dh
    _, _, gheads, dk, dv = state_gla.shape
    kw, gw = gheads * dk, gheads * dv
    n_meta = meta_tokens.shape[0]
    lyr = 0

    ffn1 = _prep_ffn(w_ffn1_in[lyr], w_ffn1_out[lyr])
    ffn2 = _prep_ffn(w_ffn2_in[lyr], w_ffn2_out[lyr])
    mix_w = _prep_mix(w_mix_in[lyr], w_gate_up[lyr], b_gate[lyr], w, kw, gw)
    wo_sb = w_out[lyr][:w].astype(BF16)
    wo_gla = w_out[lyr][w:].astype(BF16)
    g, bb = ln_g[lyr], ln_b[lyr]
    bias = b_sb[lyr].astype(F32)
    mix = functools.partial(_mix, mix_w=mix_w, w=w, kw=kw, gw=gw, sb_scale=dh ** -0.5,
                            gla_scale=dk ** -0.5)

    hm = _ffn_ln(meta_tokens.astype(x_prompt.dtype), ffn1, g[0], bb[0], alpha)
    hp = _ffn_ln(x_prompt.reshape(b * t, d), ffn1, g[0], bb[0], alpha)
    hs = _ffn_ln(x_sample.reshape(db * ts, d), ffn1, g[0], bb[0], alpha)
    _, kmf, vmf, kmb, vmb, gqm, gkm, gvm, grm, lgm = mix(hm)
    qp, kpf, vpf, kpb, vpb, gqp, gkp, gvp, grp, lgp = mix(hp)
    qs, ksf, vsf, _, _, gqs, gks, gvs, grs, lgs = mix(hs)

    r3 = lambda a, n: a.reshape(n, a.shape[0] // n, a.shape[1])
    sb_p = _sb_prompt(bias, r3(qp, b), r3(kpb, b), r3(vpb, b), kmb, vmb, dh)
    zero_state = jnp.zeros((1, gw, kw), F32)
    _, st_meta = _gla(r3(gqm, 1), r3(gkm, 1), r3(lgm, 1), r3(gvm, 1), r3(grm, 1), zero_state,
                      g_gla_norm[lyr], gheads, n_meta, n_meta)
    gla_p, st_p = _gla(r3(gqp, b), r3(gkp, b), r3(lgp, b), r3(gvp, b), r3(grp, b), st_meta,
                       g_gla_norm[lyr], gheads, GLA_CHUNK, 512)
    hp = _outproj_ln(hp, sb_p.reshape(b * t, w), gla_p.reshape(b * t, gw), wo_sb, wo_gla,
                     g[1], bb[1], alpha)

    sb_s = _sb_sample(bias, r3(qs, db).astype(F32), r3(ksf, db), r3(vsf, db),
                      cache_k[lyr].reshape(n_phys, page, w), cache_v[lyr].reshape(n_phys, page, w),
                      page_table, heads)
    ts_pad = -(-ts // GLA_SUB) * GLA_SUB
    padt = lambda a: jnp.pad(r3(a, db), ((0, 0), (0, ts_pad - ts), (0, 0)))
    gla_s, st_s = _gla(padt(gqs), padt(gks), padt(lgs), padt(gvs), padt(grs),
                       _state_to_blockdiag(state_gla[lyr].astype(F32)), g_gla_norm[lyr], gheads,
                       ts_pad, ts_pad)
    hs = _outproj_ln(hs, sb_s.reshape(db * ts, w).astype(BF16),
                     gla_s[:, :ts].reshape(db * ts, gw), wo_sb, wo_gla, g[1], bb[1], alpha)

    y_prompt = _ffn_ln(hp, ffn2, g[2], bb[2], alpha).reshape(b, t, d)
    y_sample = _ffn_ln(hs, ffn2, g[2], bb[2], alpha).reshape(db, ts, d)

    def with_meta(meta_rows, rows):
        full = jnp.concatenate([jnp.broadcast_to(meta_rows[None], (b, n_meta, w)), r3(rows, b)],
                               axis=1)
        return full.reshape(1, b, n_meta + t, heads, dh)

    sdt = state_gla.dtype
    return (y_prompt, y_sample, with_meta(kmf, kpf), with_meta(vmf, vpf),
            _state_from_blockdiag(st_p, gheads).astype(sdt)[None],
            ksf.reshape(1, db, ts, heads, dh), vsf.reshape(1, db, ts, heads, dh),
            _state_from_blockdiag(st_s, gheads).astype(sdt)[None])
```

```python
import functools

import jax
import jax.numpy as jnp
from jax import lax
from jax.experimental import pallas as pl
from jax.experimental.pallas import tpu as pltpu

F32 = jnp.float32
BF16 = jnp.bfloat16

N_META = 16
LN_EPS = 1e-5
RMS_EPS = 1e-6
GLA_TAU = 16.0
GLA_CHUNK = 64
GLA_SUB = 16
LANES = 128
KEY_BLOCK = 256
VMEM_LIMIT_BYTES = 56 * 1024 * 1024


def _dot(a, b):
    return jnp.dot(a, b, preferred_element_type=F32)


def _dot_nt(a, b):
    return lax.dot_general(a, b, (((1,), (1,)), ((), ())), preferred_element_type=F32)


def _dot_tn(a, b):
    return lax.dot_general(a, b, (((0,), (0,)), ((), ())), preferred_element_type=F32)


def _params(n_grid):
    return pltpu.CompilerParams(dimension_semantics=("arbitrary",) * n_grid,
                                vmem_limit_bytes=VMEM_LIMIT_BYTES)


def _resident(shape):
    return pl.BlockSpec(shape, lambda *_: (0,) * len(shape), pipeline_mode=pl.Buffered(1))


def _layer_norm(y, g, b):
    mu = jnp.mean(y, axis=-1, keepdims=True)
    d = y - mu
    var = jnp.mean(d * d, axis=-1, keepdims=True)
    return d * lax.rsqrt(var + LN_EPS) * g + b


def _log_sigmoid_pair(z):
    l = jnp.log(1.0 + jnp.exp(-jnp.abs(z)))
    lb = jnp.minimum(z, 0.0) - l
    return lb, lb - z


def _ffn_ln_kernel(x_ref, wg_ref, wu_ref, wo_ref, g_ref, b_ref, o_ref, acc_ref, *, alpha):
    x = x_ref[...]
    xb = x.astype(BF16)
    for c in range(wg_ref.shape[0]):
        gate = _dot(xb, wg_ref[c])
        up = _dot(xb, wu_ref[c])
        act = (gate * jax.nn.sigmoid(gate) * up).astype(BF16)
        part = _dot(act, wo_ref[c])
        if c == 0:
            acc_ref[...] = part
        else:
            acc_ref[...] += part
    o_ref[...] = _layer_norm(alpha * x + 0.5 * acc_ref[...], g_ref[...], b_ref[...])


def _prep_ffn(w_in, w_out, tf=256):
    d, f = w_in.shape[0], w_out.shape[0]
    nc = f // tf
    assert nc * tf == f
    wg = w_in[:, :f].reshape(d, nc, tf).transpose(1, 0, 2).astype(BF16)
    wu = w_in[:, f:].reshape(d, nc, tf).transpose(1, 0, 2).astype(BF16)
    wo = w_out.reshape(nc, tf, d).astype(BF16)
    return wg, wu, wo


def _ffn_ln(x, ffn_w, g, b, alpha, tm=512):
    wg, wu, wo = ffn_w
    m, d = x.shape
    tm = min(tm, m)
    assert m % tm == 0
    row = pl.BlockSpec((tm, d), lambda i: (i, 0))
    return pl.pallas_call(
        functools.partial(_ffn_ln_kernel, alpha=alpha),
        grid=(m // tm,),
        in_specs=[row, _resident(wg.shape), _resident(wu.shape), _resident(wo.shape),
                  _resident((1, d)), _resident((1, d))],
        out_specs=row,
        out_shape=jax.ShapeDtypeStruct((m, d), F32),
        scratch_shapes=[pltpu.VMEM((tm, d), F32)],
        compiler_params=_params(1),
        name="ffn_ln",
    )(x, wg, wu, wo, g.reshape(1, d), b.reshape(1, d))


def _mix_kernel(x_ref, wsb_ref, wgl_ref, wlr_ref, wgu_ref, bg_ref,
                q_ref, kf_ref, vf_ref, kb_ref, vb_ref, gq_ref, gk_ref, gv_ref, gr_ref, lg_ref,
                *, sb_scale, gla_scale):
    xb = x_ref[...].astype(BF16)
    w = q_ref.shape[1]
    kw = gq_ref.shape[1]
    gw = gv_ref.shape[1]
    q_ref[...] = (_dot(xb, wsb_ref[:, 0:w]) * sb_scale).astype(BF16)
    k = _dot(xb, wsb_ref[:, w:2 * w])
    kf_ref[...] = k
    kb_ref[...] = k.astype(BF16)
    v = _dot(xb, wsb_ref[:, 2 * w:3 * w])
    vf_ref[...] = v
    vb_ref[...] = v.astype(BF16)
    gq_ref[...] = _dot(xb, wgl_ref[:, 0:kw]) * gla_scale
    gk_ref[...] = _dot(xb, wgl_ref[:, kw:2 * kw])
    gv_ref[...] = _dot(xb, wgl_ref[:, 2 * kw:2 * kw + gw]).astype(BF16)
    gr_ref[...] = _dot(xb, wgl_ref[:, 2 * kw + gw:2 * kw + 2 * gw])
    low_rank = _dot(xb, wlr_ref[...]).astype(BF16)
    gate_logit = _dot(low_rank, wgu_ref[...]) + bg_ref[...]
    lg_ref[...] = _log_sigmoid_pair(gate_logit)[0] * (1.0 / GLA_TAU)


def _prep_mix(w_mix, w_gate_up, b_gate, w, kw, gw):
    d = w_mix.shape[0]
    rank = w_gate_up.shape[0]
    assert w_mix.shape[1] == 3 * w + 2 * kw + 2 * gw + rank and rank <= LANES
    wsb = w_mix[:, :3 * w].astype(BF16)
    wgl = w_mix[:, 3 * w:3 * w + 2 * kw + 2 * gw].astype(BF16)
    wlr = jnp.pad(w_mix[:, 3 * w + 2 * kw + 2 * gw:], ((0, 0), (0, LANES - rank))).astype(BF16)
    wgu = jnp.pad(w_gate_up, ((0, LANES - rank), (0, 0))).astype(BF16)
    return wsb, wgl, wlr, wgu, b_gate.reshape(1, kw).astype(F32)


def _mix(x, mix_w, w, kw, gw, sb_scale, gla_scale, tm=512):
    wsb, wgl, wlr, wgu, bg = mix_w
    m, d = x.shape
    tm = min(tm, m)
    assert m % tm == 0
    spec = lambda n: pl.BlockSpec((tm, n), lambda i: (i, 0))
    sds = lambda n, dt: jax.ShapeDtypeStruct((m, n), dt)
    return pl.pallas_call(
        functools.partial(_mix_kernel, sb_scale=sb_scale, gla_scale=gla_scale),
        grid=(m // tm,),
        in_specs=[spec(d), _resident(wsb.shape), _resident(wgl.shape), _resident(wlr.shape),
                  _resident(wgu.shape), _resident(bg.shape)],
        out_specs=[spec(w), spec(w), spec(w), spec(w), spec(w),
                   spec(kw), spec(kw), spec(gw), spec(gw), spec(kw)],
        out_shape=[sds(w, BF16), sds(w, F32), sds(w, F32), sds(w, BF16), sds(w, BF16),
                   sds(kw, F32), sds(kw, F32), sds(gw, BF16), sds(gw, F32), sds(kw, F32)],
        compiler_params=_params(1),
        name="mix_proj",
    )(x, wsb, wgl, wlr, wgu, bg)


def _sb_logs(z, u, mask):
    neg_abs = lax.bitcast_convert_type(
        lax.bitcast_convert_type(z, jnp.uint32) | jnp.uint32(0x80000000), F32)
    l = jnp.log(1.0 + jnp.exp(neg_abs))
    lb = jnp.minimum(z, 0.0) - l
    lk = lb - z
    if mask is not None:
        lk = jnp.where(mask, lk, 0.0)
    n = z.shape[1]
    s = _dot(lk.astype(BF16), u[:n, :n])
    return lb, lk[:, 0:1], s


def _sb_finish(logs, mask, c):
    lb, lk0, s = logs
    w = jnp.exp(lb + s + c)
    if mask is not None:
        w = jnp.where(mask, w, 0.0)
    return w.astype(BF16), c + s[:, 0:1] + lk0


def _sb_weights(z, u, mask, c):
    return _sb_finish(_sb_logs(z, u, mask), mask, c)


def _sb_prompt_kernel(bias_ref, q_ref, k_ref, v_ref, km_ref, vm_ref, u_ref, o_ref,
                      acc_ref, z_ref, w_ref, *, tq, n_meta, dh):
    kb = KEY_BLOCK
    hp = pl.program_id(1)
    i = pl.program_id(2)
    q = q_ref[0]
    lane = lax.broadcasted_iota(jnp.int32, q.shape, 1)
    zero = jnp.zeros_like(q)
    qs = jnp.concatenate([jnp.where(lane < dh, q, zero), jnp.where(lane >= dh, q, zero)], axis=0)
    row = lax.broadcasted_iota(jnp.int32, (2 * tq, 1), 0)
    bias = jnp.where(row < tq, bias_ref[2 * hp], bias_ref[2 * hp + 1])
    qpos = i * tq + jnp.where(row < tq, row, row - tq)
    u = u_ref[...]
    last_pair = (i * tq) // (2 * kb)

    def rows(ref, pair, half):
        off = pl.multiple_of(pair * (2 * kb) + half * kb, kb)
        return ref[0, pl.ds(off, kb), :]

    def logits(pair, half):
        return _dot_nt(qs, rows(k_ref, pair, half)) + bias

    def weighted_values(pair):
        return _dot(w_ref[1], rows(v_ref, pair, 1)) + _dot(w_ref[0], rows(v_ref, pair, 0))

    nm = km_ref.shape[0]
    meta_valid = lax.broadcasted_iota(jnp.int32, (2 * tq, nm), 1) < n_meta
    logs_meta = _sb_logs(_dot_nt(qs, km_ref[...]) + bias, u, meta_valid)

    col = lax.broadcasted_iota(jnp.int32, (2 * tq, kb), 1)
    visible = [last_pair * (2 * kb) + half * kb + col < qpos for half in (0, 1)]
    first_next = jnp.maximum(last_pair - 1, 0)
    logs_hi = _sb_logs(logits(last_pair, 1), u, visible[1])
    z_ref[1] = logits(first_next, 1)
    logs_lo = _sb_logs(logits(last_pair, 0), u, visible[0])
    z_ref[0] = logits(first_next, 0)
    acc_ref[...] = jnp.zeros_like(acc_ref)
    w_ref[1], c = _sb_finish(logs_hi, visible[1], jnp.zeros((2 * tq, 1), F32))
    w_ref[0], c = _sb_finish(logs_lo, visible[0], c)

    def body(n, c):
        nxt = jnp.maximum(last_pair - 2 - n, 0)
        logs_hi = _sb_logs(z_ref[1], u, None)
        pv = weighted_values(last_pair - n)
        z_ref[1] = logits(nxt, 1)
        logs_lo = _sb_logs(z_ref[0], u, None)
        z_ref[0] = logits(nxt, 0)
        acc_ref[...] += pv
        w_ref[1], c = _sb_finish(logs_hi, None, c)
        w_ref[0], c = _sb_finish(logs_lo, None, c)
        return c

    c = lax.fori_loop(0, last_pair, body, c)
    pv = weighted_values(0)
    wm, _ = _sb_finish(logs_meta, meta_valid, c)
    acc = acc_ref[...] + pv + _dot(wm, vm_ref[...])
    o_ref[0] = jnp.where(lane < dh, acc[:tq], acc[tq:]).astype(BF16)


def _suffix_matrix(n):
    r = lax.broadcasted_iota(jnp.int32, (n, n), 0)
    c = lax.broadcasted_iota(jnp.int32, (n, n), 1)
    return (r > c).astype(BF16)


def _sb_prompt(bias, q, k, v, km, vm, dh, tq=KEY_BLOCK):
    b, t, w = q.shape
    assert t % (2 * KEY_BLOCK) == 0 and (2 * KEY_BLOCK) % tq == 0
    assert w % LANES == 0 and LANES == 2 * dh
    n_meta = km.shape[0]
    pad = ((0, LANES - n_meta), (0, 0))
    km = jnp.pad(km, pad)
    vm = jnp.pad(vm, pad)
    u = _suffix_matrix(KEY_BLOCK)
    qspec = pl.BlockSpec((1, tq, LANES), lambda bi, hp, i: (bi, i, hp))
    kvspec = pl.BlockSpec((1, t, LANES), lambda bi, hp, i: (bi, 0, hp))
    mspec = pl.BlockSpec((LANES, LANES), lambda bi, hp, i: (0, hp))
    return pl.pallas_call(
        functools.partial(_sb_prompt_kernel, tq=tq, n_meta=n_meta, dh=dh),
        grid=(b, w // LANES, t // tq),
        in_specs=[pl.BlockSpec(memory_space=pltpu.SMEM), qspec, kvspec, kvspec, mspec, mspec,
                  _resident(u.shape)],
        out_specs=qspec,
        out_shape=jax.ShapeDtypeStruct((b, t, w), BF16),
        scratch_shapes=[pltpu.VMEM((2 * tq, LANES), F32),
                        pltpu.VMEM((2, 2 * tq, KEY_BLOCK), F32),
                        pltpu.VMEM((2, 2 * tq, KEY_BLOCK), BF16)],
        compiler_params=_params(3),
        name="sb_prompt",
    )(bias, q, k, v, km, vm, u)


def _sb_sample_kernel(pt_ref, bias_ref, q_ref, kn_ref, vn_ref, *rest, n_group, heads):
    kp = rest[:n_group]
    vp = rest[n_group:2 * n_group]
    u_ref, o_ref, qbd_ref, c_ref, acc_ref = rest[2 * n_group:]
    p = pl.program_id(1)
    t_new, w = q_ref.shape[1], q_ref.shape[2]
    dh = w // heads
    ht = heads * t_new
    page = kp[0].shape[2]
    head_of_row = lax.broadcasted_iota(jnp.int32, (ht, w), 0) // t_new
    head_of_lane = lax.broadcasted_iota(jnp.int32, (ht, w), 1) // dh
    own = head_of_row == head_of_lane
    u = u_ref[...]
    bias = bias_ref[...]

    @pl.when(p == 0)
    def _start():
        qt = jnp.concatenate([q_ref[0]] * heads, axis=0)
        qbd = jnp.where(own, qt, 0.0).astype(BF16)
        qbd_ref[...] = qbd
        fill = jnp.zeros((LANES - t_new, w), F32)
        kn = jnp.concatenate([kn_ref[0], fill], axis=0).astype(BF16)
        vn = jnp.concatenate([vn_ref[0], fill], axis=0).astype(BF16)
        col = lax.broadcasted_iota(jnp.int32, (ht, LANES), 1)
        t_of_row = lax.broadcasted_iota(jnp.int32, (ht, LANES), 0) % t_new
        wn, c_ref[...] = _sb_weights(_dot_nt(qbd, kn) + bias, u, col < t_of_row,
                                     jnp.zeros((ht, 1), F32))
        acc_ref[...] = _dot(wn, vn)

    qbd = qbd_ref[...]
    per_block = KEY_BLOCK // page
    blocks = [range(blk * per_block, (blk + 1) * per_block)
              for blk in reversed(range(n_group // per_block))]
    logs = [_sb_logs(jnp.concatenate([_dot(qbd, kp[g][0].astype(BF16)) for g in ids], axis=1)
                     + bias, u, None) for ids in blocks]
    c = c_ref[...]
    pv = None
    for ids, blk_logs in zip(blocks, logs):
        wb, c = _sb_finish(blk_logs, None, c)
        for n, g in enumerate(ids):
            part = _dot_nt(wb[:, n * page:(n + 1) * page], vp[g][0].astype(BF16))
            pv = part if pv is None else pv + part
    acc_ref[...] += pv
    c_ref[...] = c

    @pl.when(p == pl.num_programs(1) - 1)
    def _finish():
        acc = jnp.where(own, acc_ref[...], 0.0)
        out = acc[0:t_new]
        for h in range(1, heads):
            out = out + acc[h * t_new:(h + 1) * t_new]
        o_ref[0] = out


def _sb_sample(bias, q, k_new, v_new, cache_k, cache_v, page_table, heads):
    db, t_new, w = q.shape
    n_phys, _, page = cache_k.shape
    n_pages = page_table.shape[1]
    assert KEY_BLOCK % page == 0 and t_new <= LANES
    per_block = KEY_BLOCK // page
    n_group = 8 if n_pages % 8 == 0 else per_block
    assert n_pages % n_group == 0 and n_group % per_block == 0
    n_steps = n_pages // n_group
    ht = heads * t_new
    bias_rows = jnp.repeat(bias.astype(F32), t_new).reshape(ht, 1)
    u = _suffix_matrix(KEY_BLOCK)
    tok = pl.BlockSpec((1, t_new, w), lambda b, p, pt: (b, 0, 0))

    def page_spec(g):
        return pl.BlockSpec((1, w, page),
                            lambda b, p, pt: (pt[b, (n_steps - 1 - p) * n_group + g], 0, 0))

    const = lambda shape: pl.BlockSpec(shape, lambda b, p, pt: (0,) * len(shape))
    grid_spec = pltpu.PrefetchScalarGridSpec(
        num_scalar_prefetch=1,
        grid=(db, n_steps),
        in_specs=[const((ht, 1)), tok, tok, tok]
                 + [page_spec(g) for g in range(n_group)] * 2 + [const(u.shape)],
        out_specs=tok,
        scratch_shapes=[pltpu.VMEM((ht, w), BF16), pltpu.VMEM((ht, 1), F32),
                        pltpu.VMEM((ht, w), F32)],
    )
    return pl.pallas_call(
        functools.partial(_sb_sample_kernel, n_group=n_group, heads=heads),
        grid_spec=grid_spec,
        out_shape=jax.ShapeDtypeStruct((db, t_new, w), F32),
        compiler_params=_params(2),
        name="sb_sample",
    )(page_table, bias_rows, q, k_new, v_new, *([cache_k] * n_group), *([cache_v] * n_group), u)


def _gla_kernel(q_ref, k_ref, lg_ref, v_ref, r_ref, s0_ref, gn_ref, ltri_ref, y_ref, sout_ref,
                st_ref, *, chunk, sub, heads):
    t = pl.program_id(1)
    tb, kw = q_ref.shape[1], q_ref.shape[2]
    gw = v_ref.shape[2]
    dk, dv = kw // heads, gw // heads
    n_sub = chunk // sub
    n_chunks = tb // chunk
    group = 8 if n_chunks % 8 == 0 else 1

    @pl.when(t == 0)
    def _init():
        st_ref[...] = s0_ref[0]

    own_state = (lax.broadcasted_iota(jnp.int32, (gw, kw), 0) // dv
                 == lax.broadcasted_iota(jnp.int32, (gw, kw), 1) // dk)
    lane_head = lax.broadcasted_iota(jnp.int32, (sub, kw), 1) // dk
    srow = lax.broadcasted_iota(jnp.int32, (chunk, kw), 0)
    att_col = lax.broadcasted_iota(jnp.int32, (heads * sub, chunk), 1)
    att_t = lax.broadcasted_iota(jnp.int32, (heads * sub, chunk), 0) % sub
    ltri = ltri_ref[...]
    gn = gn_ref[...]

    def state_free(rows_list):
        n = range(len(rows_list))
        q = [q_ref[0, r, :] for r in rows_list]
        k = [k_ref[0, r, :] for r in rows_list]
        v = [v_ref[0, r, :] for r in rows_list]
        lg = [lg_ref[0, r, :] for r in rows_list]
        lg_hi = [x.astype(BF16) for x in lg]
        lg_lo = [(x - h.astype(F32)).astype(BF16) for x, h in zip(lg, lg_hi)]
        bc = [_dot(ltri, h) + _dot(ltri, l) for h, l in zip(lg_hi, lg_lo)]
        b_last = [x[chunk - 1:chunk, :] for x in bc]
        k_last = [(k[j] * jnp.exp(b_last[j] - bc[j])).astype(BF16) for j in n]
        update = [jnp.where(own_state, _dot_tn(v[j], k_last[j]), 0.0) for j in n]
        q_decayed = [(q[j] * jnp.exp(bc[j])).astype(BF16) for j in n]
        intra = [[] for _ in n]
        for sc in range(n_sub):
            lo, hi = sc * sub, (sc + 1) * sub
            seen = srow < hi
            att = []
            for j in n:
                ref = bc[j][lo - 1:lo, :] if sc > 0 else jnp.zeros((1, kw), F32)
                qd = q[j][lo:hi] * jnp.exp(bc[j][lo:hi] - ref)
                kd = (jnp.where(seen, k[j], 0.0)
                      * jnp.exp(jnp.where(seen, ref - bc[j], 0.0))).astype(BF16)
                qs = jnp.concatenate([jnp.where(lane_head == h, qd, 0.0) for h in range(heads)],
                                     axis=0).astype(BF16)
                att.append(_dot_nt(qs, kd))
            for j in n:
                a = jnp.where(att_col <= lo + att_t, att[j], 0.0).astype(BF16)
                ov = _dot(a, v[j])
                intra[j].append(jnp.concatenate(
                    [ov[h * sub:(h + 1) * sub, h * dv:(h + 1) * dv] for h in range(heads)],
                    axis=1))
        return [(q_decayed[j], jnp.concatenate(intra[j], axis=0), jnp.exp(b_last[j]), update[j])
                for j in n]

    def finish(rows_list, outs):
        heads_of = [[o[:, h * dv:(h + 1) * dv] for h in range(heads)] for o in outs]
        scale = [[lax.rsqrt(jnp.mean(oh * oh, axis=-1, keepdims=True) + RMS_EPS) for oh in ohs]
                 for ohs in heads_of]
        for rows, ohs, scs in zip(rows_list, heads_of, scale):
            r = r_ref[0, rows, :]
            normed = jnp.concatenate([oh * sc for oh, sc in zip(ohs, scs)], axis=1)
            y_ref[0, rows, :] = (normed * gn * (r * jax.nn.sigmoid(r))).astype(BF16)

    def group_body(gi, carry):
        rows = [pl.ds(pl.multiple_of((gi * group + j) * chunk, chunk), chunk)
                for j in range(group)]
        parts = state_free(rows)
        st = st_ref[...]
        outs = []
        for q_decayed, intra, decay, update in parts:
            outs.append(_dot_nt(q_decayed, st.astype(BF16)) + intra)
            st = st * decay + update
        st_ref[...] = st
        finish(rows, outs)
        return carry

    lax.fori_loop(0, tb // (chunk * group), group_body, 0)

    @pl.when(t == pl.num_programs(1) - 1)
    def _final():
        sout_ref[0] = st_ref[...]


def _gla(q, k, lg, v, r, st0, g_norm, heads, chunk, tb):
    b, t, kw = q.shape
    gw = v.shape[2]
    sub = min(GLA_SUB, chunk)
    tb = min(tb, t)
    assert t % tb == 0 and tb % chunk == 0 and chunk % sub == 0
    shared_state = st0.shape[0] == 1
    ltri = (lax.broadcasted_iota(jnp.int32, (chunk, chunk), 0)
            >= lax.broadcasted_iota(jnp.int32, (chunk, chunk), 1)).astype(BF16)
    gn = jnp.tile(g_norm.astype(F32), heads).reshape(1, gw)
    tok = lambda n: pl.BlockSpec((1, tb, n), lambda bi, ti: (bi, ti, 0))
    state_in = pl.BlockSpec((1, gw, kw), lambda bi, ti: (0 if shared_state else bi, 0, 0))
    state_out = pl.BlockSpec((1, gw, kw), lambda bi, ti: (bi, 0, 0))
    const = lambda shape: pl.BlockSpec(shape, lambda bi, ti: (0,) * len(shape))
    return pl.pallas_call(
        functools.partial(_gla_kernel, chunk=chunk, sub=sub, heads=heads),
        grid=(b, t // tb),
        in_specs=[tok(kw), tok(kw), tok(kw), tok(gw), tok(gw), state_in, const((1, gw)),
                  const((chunk, chunk))],
        out_specs=[tok(gw), state_out],
        out_shape=[jax.ShapeDtypeStruct((b, t, gw), BF16),
                   jax.ShapeDtypeStruct((b, gw, kw), F32)],
        scratch_shapes=[pltpu.VMEM((gw, kw), F32)],
        compiler_params=_params(2),
        name="gla",
    )(q, k, lg, v, r, st0, gn, ltri)


def _state_to_blockdiag(s):
    n, h, dk, dv = s.shape
    eye = jnp.eye(h, dtype=s.dtype)
    st = s.transpose(0, 1, 3, 2)[:, :, :, None, :] * eye[None, :, None, :, None]
    return st.reshape(n, h * dv, h * dk)


def _state_from_blockdiag(st, h):
    n, gw, kw = st.shape
    dv, dk = gw // h, kw // h
    st5 = st.reshape(n, h, dv, h, dk)
    return jnp.stack([st5[:, i, :, i, :] for i in range(h)], axis=1).transpose(0, 1, 3, 2)


def _outproj_ln_kernel(h_ref, sb_ref, gl_ref, w1_ref, w2_ref, g_ref, b_ref, o_ref, *, alpha):
    sub = _dot(sb_ref[...], w1_ref[...]) + _dot(gl_ref[...], w2_ref[...])
    o_ref[...] = _layer_norm(alpha * h_ref[...] + sub, g_ref[...], b_ref[...])


def _outproj_ln(h, sb_o, gla_y, w1, w2, g, b, alpha, tm=512):
    m, d = h.shape
    tm = min(tm, m)
    assert m % tm == 0
    spec = lambda n: pl.BlockSpec((tm, n), lambda i: (i, 0))
    return pl.pallas_call(
        functools.partial(_outproj_ln_kernel, alpha=alpha),
        grid=(m // tm,),
        in_specs=[spec(d), spec(sb_o.shape[1]), spec(gla_y.shape[1]), _resident(w1.shape),
                  _resident(w2.shape), _resident((1, d)), _resident((1, d))],
        out_specs=spec(d),
        out_shape=jax.ShapeDtypeStruct((m, d), F32),
        compiler_params=_params(1),
        name="outproj_ln",
    )(h, sb_o, gla_y, w1, w2, g.reshape(1, d), b.reshape(1, d))


def kernel(x_prompt, x_sample, cache_k, cache_v, state_gla, page_table, meta_tokens, ln_g, ln_b,
           w_ffn1_in, w_ffn1_out, w_mix_in, w_gate_up, b_gate, b_sb, g_gla_norm, w_out,
           w_ffn2_in, w_ffn2_out):
    depth = w_out.shape[0]
    assert depth == 1
    alpha = (2.0 * depth) ** 0.25
    b, t, d = x_prompt.shape
    db, ts, _ = x_sample.shape
    _, n_phys, page, heads, dh = cache_k.shape
    w = heads * dh
    _, _, gheads, dk, dv = state_gla.shape
    kw, gw = gheads * dk, gheads * dv
    n_meta = meta_tokens.shape[0]
    lyr = 0

    ffn1 = _prep_ffn(w_ffn1_in[lyr], w_ffn1_out[lyr])
    ffn2 = _prep_ffn(w_ffn2_in[lyr], w_ffn2_out[lyr])
    mix_w = _prep_mix(w_mix_in[lyr], w_gate_up[lyr], b_gate[lyr], w, kw, gw)
    wo_sb = w_out[lyr][:w].astype(BF16)
    wo_gla = w_out[lyr][w:].astype(BF16)
    g, bb = ln_g[lyr], ln_b[lyr]
    bias = b_sb[lyr].astype(F32)
    mix = functools.partial(_mix, mix_w=mix_w, w=w, kw=kw, gw=gw, sb_scale=dh ** -0.5,
                            gla_scale=dk ** -0.5)

    hm = _ffn_ln(meta_tokens.astype(x_prompt.dtype), ffn1, g[0], bb[0], alpha)
    hp = _ffn_ln(x_prompt.reshape(b * t, d), ffn1, g[0], bb[0], alpha)
    hs = _ffn_ln(x_sample.reshape(db * ts, d), ffn1, g[0], bb[0], alpha)
    _, kmf, vmf, kmb, vmb, gqm, gkm, gvm, grm, lgm = mix(hm)
    qp, kpf, vpf, kpb, vpb, gqp, gkp, gvp, grp, lgp = mix(hp)
    qs, ksf, vsf, _, _, gqs, gks, gvs, grs, lgs = mix(hs)

    r3 = lambda a, n: a.reshape(n, a.shape[0] // n, a.shape[1])
    sb_p = _sb_prompt(bias, r3(qp, b), r3(kpb, b), r3(vpb, b), kmb, vmb, dh)
    zero_state = jnp.zeros((1, gw, kw), F32)
    _, st_meta = _gla(r3(gqm, 1), r3(gkm, 1), r3(lgm, 1), r3(gvm, 1), r3(grm, 1), zero_state,
                      g_gla_norm[lyr], gheads, n_meta, n_meta)
    gla_p, st_p = _gla(r3(gqp, b), r3(gkp, b), r3(lgp, b), r3(gvp, b), r3(grp, b), st_meta,
                       g_gla_norm[lyr], gheads, GLA_CHUNK, 512)
    hp = _outproj_ln(hp, sb_p.reshape(b * t, w), gla_p.reshape(b * t, gw), wo_sb, wo_gla,
                     g[1], bb[1], alpha)

    feature_major = lambda c: c.transpose(0, 2, 3, 1).reshape(n_phys, w, page)
    sb_s = _sb_sample(bias, r3(qs, db).astype(F32), r3(ksf, db), r3(vsf, db),
                      feature_major(cache_k[lyr]), feature_major(cache_v[lyr]), page_table, heads)
    ts_pad = -(-ts // GLA_SUB) * GLA_SUB
    padt = lambda a: jnp.pad(r3(a, db), ((0, 0), (0, ts_pad - ts), (0, 0)))
    gla_s, st_s = _gla(padt(gqs), padt(gks), padt(lgs), padt(gvs), padt(grs),
                       _state_to_blockdiag(state_gla[lyr].astype(F32)), g_gla_norm[lyr], gheads,
                       ts_pad, ts_pad)
    hs = _outproj_ln(hs, sb_s.reshape(db * ts, w).astype(BF16),
                     gla_s[:, :ts].reshape(db * ts, gw), wo_sb, wo_gla, g[1], bb[1], alpha)

    y_prompt = _ffn_ln(hp, ffn2, g[2], bb[2], alpha).reshape(b, t, d)
    y_sample = _ffn_ln(hs, ffn2, g[2], bb[2], alpha).reshape(db, ts, d)

    def with_meta(meta_rows, rows):
        full = jnp.concatenate([jnp.broadcast_to(meta_rows[None], (b, n_meta, w)), r3(rows, b)],
                               axis=1)
        return full.reshape(1, b, n_meta + t, heads, dh)

    sdt = state_gla.dtype
    return (y_prompt, y_sample, with_meta(kmf, kpf), with_meta(vmf, vpf),
            _state_from_blockdiag(st_p, gheads).astype(sdt)[None],
            ksf.reshape(1, db, ts, heads, dh), vsf.reshape(1, db, ts, heads, dh),
            _state_from_blockdiag(st_s, gheads).astype(sdt)[None])
```

```python
import functools

import jax
import jax.numpy as jnp
from jax import lax
from jax.experimental import pallas as pl
from jax.experimental.pallas import tpu as pltpu

F32 = jnp.float32
BF16 = jnp.bfloat16

LN_EPS = 1e-5
RMS_EPS = 1e-6
GLA_TAU = 16.0
GLA_CHUNK = 64
GLA_SUB = 16
LANES = 128
KEY_BLOCK = 256
LOG2E = 1.4426950408889634
VMEM_LIMIT_BYTES = 56 * 1024 * 1024


def _dot(a, b):
    return jnp.dot(a, b, preferred_element_type=F32)


def _dot_nt(a, b):
    return lax.dot_general(a, b, (((1,), (1,)), ((), ())), preferred_element_type=F32)


def _dot_tn(a, b):
    return lax.dot_general(a, b, (((0,), (0,)), ((), ())), preferred_element_type=F32)


def _params(n_grid):
    return pltpu.CompilerParams(dimension_semantics=("arbitrary",) * n_grid,
                                vmem_limit_bytes=VMEM_LIMIT_BYTES)


def _resident(shape):
    return pl.BlockSpec(shape, lambda *_: (0,) * len(shape), pipeline_mode=pl.Buffered(1))


def _layer_norm(y, g, b):
    mu = jnp.mean(y, axis=-1, keepdims=True)
    d = y - mu
    var = jnp.mean(d * d, axis=-1, keepdims=True)
    return d * lax.rsqrt(var + LN_EPS) * g + b


def _log_sigmoid(z):
    return jnp.minimum(z, 0.0) - jnp.log(1.0 + jnp.exp(-jnp.abs(z)))


def _swiglu_ln(x, wg_ref, wu_ref, wo_ref, g, b, acc_ref, alpha):
    xb = x.astype(BF16)
    for c in range(wg_ref.shape[0]):
        gate = _dot(xb, wg_ref[c])
        up = _dot(xb, wu_ref[c])
        act = (gate * jax.nn.sigmoid(gate) * up).astype(BF16)
        part = _dot(act, wo_ref[c])
        if c == 0:
            acc_ref[...] = part
        else:
            acc_ref[...] += part
    return _layer_norm(alpha * x + 0.5 * acc_ref[...], g, b)


def _ffn_ln_kernel(x_ref, wg_ref, wu_ref, wo_ref, g_ref, b_ref, o_ref, acc_ref, *, alpha):
    o_ref[...] = _swiglu_ln(x_ref[...], wg_ref, wu_ref, wo_ref, g_ref[...], b_ref[...], acc_ref,
                            alpha)


def _outproj_ffn_ln_kernel(h_ref, sb_ref, gl_ref, w1_ref, w2_ref, g1_ref, b1_ref,
                           wg_ref, wu_ref, wo_ref, g2_ref, b2_ref, o_ref, acc_ref, *, alpha):
    mixed = _dot(sb_ref[...], w1_ref[...]) + _dot(gl_ref[...], w2_ref[...])
    x = _layer_norm(alpha * h_ref[...] + mixed, g1_ref[...], b1_ref[...])
    o_ref[...] = _swiglu_ln(x, wg_ref, wu_ref, wo_ref, g2_ref[...], b2_ref[...], acc_ref, alpha)


def _prep_ffn(w_in, w_out, tf=256):
    d, f = w_in.shape[0], w_out.shape[0]
    nc = f // tf
    assert nc * tf == f
    wg = w_in[:, :f].reshape(d, nc, tf).transpose(1, 0, 2).astype(BF16)
    wu = w_in[:, f:].reshape(d, nc, tf).transpose(1, 0, 2).astype(BF16)
    wo = w_out.reshape(nc, tf, d).astype(BF16)
    return wg, wu, wo


def _ffn_ln(x, ffn_w, g, b, alpha, tm=512):
    wg, wu, wo = ffn_w
    m, d = x.shape
    tm = min(tm, m)
    assert m % tm == 0
    row = pl.BlockSpec((tm, d), lambda i: (i, 0))
    return pl.pallas_call(
        functools.partial(_ffn_ln_kernel, alpha=alpha),
        grid=(m // tm,),
        in_specs=[row, _resident(wg.shape), _resident(wu.shape), _resident(wo.shape),
                  _resident((1, d)), _resident((1, d))],
        out_specs=row,
        out_shape=jax.ShapeDtypeStruct((m, d), F32),
        scratch_shapes=[pltpu.VMEM((tm, d), F32)],
        compiler_params=_params(1),
        name="ffn_ln",
    )(x, wg, wu, wo, g.reshape(1, d), b.reshape(1, d))


def _outproj_ffn_ln(h, sb_o, gla_y, w1, w2, g1, b1, ffn_w, g2, b2, alpha, tm=512):
    wg, wu, wo = ffn_w
    m, d = h.shape
    tm = min(tm, m)
    assert m % tm == 0
    spec = lambda n: pl.BlockSpec((tm, n), lambda i: (i, 0))
    vec = lambda a: a.reshape(1, d)
    return pl.pallas_call(
        functools.partial(_outproj_ffn_ln_kernel, alpha=alpha),
        grid=(m // tm,),
        in_specs=[spec(d), spec(sb_o.shape[1]), spec(gla_y.shape[1]), _resident(w1.shape),
                  _resident(w2.shape), _resident((1, d)), _resident((1, d)),
                  _resident(wg.shape), _resident(wu.shape), _resident(wo.shape),
                  _resident((1, d)), _resident((1, d))],
        out_specs=spec(d),
        out_shape=jax.ShapeDtypeStruct((m, d), F32),
        scratch_shapes=[pltpu.VMEM((tm, d), F32)],
        compiler_params=_params(1),
        name="outproj_ffn_ln",
    )(h, sb_o, gla_y, w1, w2, vec(g1), vec(b1), wg, wu, wo, vec(g2), vec(b2))


def _mix_kernel(x_ref, wsb_ref, wgl_ref, wlr_ref, wgu_ref, bg_ref,
                q_ref, kf_ref, vf_ref, kb_ref, vb_ref, gq_ref, gk_ref, gv_ref, gr_ref, lg_ref,
                *, sb_scale, gla_scale):
    xb = x_ref[...].astype(BF16)
    w = q_ref.shape[1]
    kw = gq_ref.shape[1]
    gw = gv_ref.shape[1]
    q_ref[...] = (_dot(xb, wsb_ref[:, 0:w]) * sb_scale).astype(BF16)
    k = _dot(xb, wsb_ref[:, w:2 * w])
    kf_ref[...] = k
    kb_ref[...] = k.astype(BF16)
    v = _dot(xb, wsb_ref[:, 2 * w:3 * w])
    vf_ref[...] = v
    vb_ref[...] = v.astype(BF16)
    gq_ref[...] = _dot(xb, wgl_ref[:, 0:kw]) * gla_scale
    gk_ref[...] = _dot(xb, wgl_ref[:, kw:2 * kw])
    gv_ref[...] = _dot(xb, wgl_ref[:, 2 * kw:2 * kw + gw]).astype(BF16)
    gr_ref[...] = _dot(xb, wgl_ref[:, 2 * kw + gw:2 * kw + 2 * gw])
    low_rank = _dot(xb, wlr_ref[...]).astype(BF16)
    gate_logit = _dot(low_rank, wgu_ref[...]) + bg_ref[...]
    lg_ref[...] = _log_sigmoid(gate_logit) * (1.0 / GLA_TAU)


def _prep_mix(w_mix, w_gate_up, b_gate, w, kw, gw):
    rank = w_gate_up.shape[0]
    assert w_mix.shape[1] == 3 * w + 2 * kw + 2 * gw + rank and rank <= LANES
    wsb = w_mix[:, :3 * w].astype(BF16)
    wgl = w_mix[:, 3 * w:3 * w + 2 * kw + 2 * gw].astype(BF16)
    wlr = jnp.pad(w_mix[:, 3 * w + 2 * kw + 2 * gw:], ((0, 0), (0, LANES - rank))).astype(BF16)
    wgu = jnp.pad(w_gate_up, ((0, LANES - rank), (0, 0))).astype(BF16)
    return wsb, wgl, wlr, wgu, b_gate.reshape(1, kw).astype(F32)


def _mix(x, mix_w, w, kw, gw, sb_scale, gla_scale, tm=512):
    wsb, wgl, wlr, wgu, bg = mix_w
    m, d = x.shape
    tm = min(tm, m)
    assert m % tm == 0
    spec = lambda n: pl.BlockSpec((tm, n), lambda i: (i, 0))
    sds = lambda n, dt: jax.ShapeDtypeStruct((m, n), dt)
    return pl.pallas_call(
        functools.partial(_mix_kernel, sb_scale=sb_scale, gla_scale=gla_scale),
        grid=(m // tm,),
        in_specs=[spec(d), _resident(wsb.shape), _resident(wgl.shape), _resident(wlr.shape),
                  _resident(wgu.shape), _resident(bg.shape)],
        out_specs=[spec(w), spec(w), spec(w), spec(w), spec(w),
                   spec(kw), spec(kw), spec(gw), spec(gw), spec(kw)],
        out_shape=[sds(w, BF16), sds(w, F32), sds(w, F32), sds(w, BF16), sds(w, BF16),
                   sds(kw, F32), sds(kw, F32), sds(gw, BF16), sds(gw, F32), sds(kw, F32)],
        compiler_params=_params(1),
        name="mix_proj",
    )(x, wsb, wgl, wlr, wgu, bg)


def _sb_logs(z2, u, mask):
    neg_abs = lax.bitcast_convert_type(
        lax.bitcast_convert_type(z2, jnp.uint32) | jnp.uint32(0x80000000), F32)
    l2 = jnp.log(1.0 + jnp.exp2(neg_abs)) * LOG2E
    lb = jnp.minimum(z2, 0.0) - l2
    lk = lb - z2
    if mask is not None:
        lk = jnp.where(mask, lk, 0.0)
    n = z2.shape[1]
    s = _dot(lk.astype(BF16), u[:n, :n])
    return lb, lk[:, 0:1], s


def _sb_finish(logs, mask, c):
    lb, lk0, s = logs
    w = jnp.exp2(lb + s + c)
    if mask is not None:
        w = jnp.where(mask, w, 0.0)
    return w.astype(BF16), c + s[:, 0:1] + lk0


def _sb_prompt_kernel(bias_ref, q_ref, k_ref, v_ref, km_ref, vm_ref, u_ref, o_ref,
                      acc_ref, z_ref, w_ref, *, tq, n_meta, dh):
    kb = KEY_BLOCK
    hp = pl.program_id(1)
    i = pl.program_id(2)
    q = q_ref[0]
    lane = lax.broadcasted_iota(jnp.int32, q.shape, 1)
    zero = jnp.zeros_like(q)
    qs = jnp.concatenate([jnp.where(lane < dh, q, zero), jnp.where(lane >= dh, q, zero)], axis=0)
    row = lax.broadcasted_iota(jnp.int32, (2 * tq, 1), 0)
    bias = jnp.where(row < tq, bias_ref[2 * hp], bias_ref[2 * hp + 1]) * LOG2E
    qpos = i * tq + jnp.where(row < tq, row, row - tq)
    u = u_ref[...]
    last_pair = (i * tq) // (2 * kb)

    def rows(ref, pair, half):
        off = pl.multiple_of(pair * (2 * kb) + half * kb, kb)
        return ref[0, pl.ds(off, kb), :]

    def logits(pair, half):
        return _dot_nt(qs, rows(k_ref, pair, half)) + bias

    def weighted_values(pair):
        return _dot(w_ref[1], rows(v_ref, pair, 1)) + _dot(w_ref[0], rows(v_ref, pair, 0))

    nm = km_ref.shape[0]
    meta_valid = lax.broadcasted_iota(jnp.int32, (2 * tq, nm), 1) < n_meta
    logs_meta = _sb_logs(_dot_nt(qs, km_ref[...]) + bias, u, meta_valid)

    col = lax.broadcasted_iota(jnp.int32, (2 * tq, kb), 1)
    visible = [last_pair * (2 * kb) + half * kb + col < qpos for half in (0, 1)]
    first_next = jnp.maximum(last_pair - 1, 0)
    logs_hi = _sb_logs(logits(last_pair, 1), u, visible[1])
    z_ref[1] = logits(first_next, 1)
    logs_lo = _sb_logs(logits(last_pair, 0), u, visible[0])
    z_ref[0] = logits(first_next, 0)
    acc_ref[...] = jnp.zeros_like(acc_ref)
    w_ref[1], c = _sb_finish(logs_hi, visible[1], jnp.zeros((2 * tq, 1), F32))
    w_ref[0], c = _sb_finish(logs_lo, visible[0], c)

    def body(n, c):
        nxt = jnp.maximum(last_pair - 2 - n, 0)
        logs_hi = _sb_logs(z_ref[1], u, None)
        pv = weighted_values(last_pair - n)
        z_ref[1] = logits(nxt, 1)
        logs_lo = _sb_logs(z_ref[0], u, None)
        z_ref[0] = logits(nxt, 0)
        acc_ref[...] += pv
        w_ref[1], c = _sb_finish(logs_hi, None, c)
        w_ref[0], c = _sb_finish(logs_lo, None, c)
        return c

    c = lax.fori_loop(0, last_pair // 2, lambda m, c: body(2 * m + 1, body(2 * m, c)), c)
    c = lax.cond(last_pair % 2 == 1, lambda c: body(last_pair - 1, c), lambda c: c, c)
    pv = weighted_values(0)
    wm, _ = _sb_finish(logs_meta, meta_valid, c)
    acc = acc_ref[...] + pv + _dot(wm, vm_ref[...])
    o_ref[0] = jnp.where(lane < dh, acc[:tq], acc[tq:]).astype(BF16)


def _suffix_matrix(n):
    r = lax.broadcasted_iota(jnp.int32, (n, n), 0)
    c = lax.broadcasted_iota(jnp.int32, (n, n), 1)
    return (r > c).astype(BF16)


def _sb_prompt(bias, q, k, v, km, vm, dh, tq=KEY_BLOCK):
    b, t, w = q.shape
    assert t % (2 * KEY_BLOCK) == 0 and (2 * KEY_BLOCK) % tq == 0
    assert w % LANES == 0 and LANES == 2 * dh
    n_meta = km.shape[0]
    pad = ((0, LANES - n_meta), (0, 0))
    km = jnp.pad(km, pad)
    vm = jnp.pad(vm, pad)
    u = _suffix_matrix(KEY_BLOCK)
    qspec = pl.BlockSpec((1, tq, LANES), lambda bi, hp, i: (bi, i, hp))
    kvspec = pl.BlockSpec((1, t, LANES), lambda bi, hp, i: (bi, 0, hp))
    mspec = pl.BlockSpec((LANES, LANES), lambda bi, hp, i: (0, hp))
    return pl.pallas_call(
        functools.partial(_sb_prompt_kernel, tq=tq, n_meta=n_meta, dh=dh),
        grid=(b, w // LANES, t // tq),
        in_specs=[pl.BlockSpec(memory_space=pltpu.SMEM), qspec, kvspec, kvspec, mspec, mspec,
                  _resident(u.shape)],
        out_specs=qspec,
        out_shape=jax.ShapeDtypeStruct((b, t, w), BF16),
        scratch_shapes=[pltpu.VMEM((2 * tq, LANES), F32),
                        pltpu.VMEM((2, 2 * tq, KEY_BLOCK), F32),
                        pltpu.VMEM((2, 2 * tq, KEY_BLOCK), BF16)],
        compiler_params=_params(3),
        name="sb_prompt",
    )(bias, q, k, v, km, vm, u)


def _sb_sample_kernel(pt_ref, bias_ref, q_ref, kn_ref, vn_ref, *rest, n_group, heads):
    kp = rest[:n_group]
    vp = rest[n_group:2 * n_group]
    u_ref, o_ref, qbd_ref, c_ref, acc_ref = rest[2 * n_group:]
    p = pl.program_id(1)
    t_new, w = q_ref.shape[1], q_ref.shape[2]
    dh = w // heads
    ht = heads * t_new
    page = kp[0].shape[2]
    head_of_row = lax.broadcasted_iota(jnp.int32, (ht, w), 0) // t_new
    head_of_lane = lax.broadcasted_iota(jnp.int32, (ht, w), 1) // dh
    own = head_of_row == head_of_lane
    u = u_ref[...]
    bias = bias_ref[...] * LOG2E

    @pl.when(p == 0)
    def _start():
        qt = jnp.concatenate([q_ref[0]] * heads, axis=0)
        qbd = jnp.where(own, qt, 0.0).astype(BF16)
        qbd_ref[...] = qbd
        fill = jnp.zeros((LANES - t_new, w), F32)
        kn = jnp.concatenate([kn_ref[0], fill], axis=0).astype(BF16)
        vn = jnp.concatenate([vn_ref[0], fill], axis=0).astype(BF16)
        col = lax.broadcasted_iota(jnp.int32, (ht, LANES), 1)
        t_of_row = lax.broadcasted_iota(jnp.int32, (ht, LANES), 0) % t_new
        visible = col < t_of_row
        wn, c_ref[...] = _sb_finish(_sb_logs(_dot_nt(qbd, kn) + bias, u, visible), visible,
                                    jnp.zeros((ht, 1), F32))
        acc_ref[...] = _dot(wn, vn)

    qbd = qbd_ref[...]
    per_block = KEY_BLOCK // page
    blocks = [range(blk * per_block, (blk + 1) * per_block)
              for blk in reversed(range(n_group // per_block))]
    logs = [_sb_logs(jnp.concatenate([_dot(qbd, kp[g][0].astype(BF16)) for g in ids], axis=1)
                     + bias, u, None) for ids in blocks]
    c = c_ref[...]
    pv = None
    for ids, blk_logs in zip(blocks, logs):
        wb, c = _sb_finish(blk_logs, None, c)
        for n, g in enumerate(ids):
            part = _dot_nt(wb[:, n * page:(n + 1) * page], vp[g][0].astype(BF16))
            pv = part if pv is None else pv + part
    acc_ref[...] += pv
    c_ref[...] = c

    @pl.when(p == pl.num_programs(1) - 1)
    def _finish():
        acc = jnp.where(own, acc_ref[...], 0.0)
        out = acc[0:t_new]
        for h in range(1, heads):
            out = out + acc[h * t_new:(h + 1) * t_new]
        o_ref[0] = out


def _sb_sample(bias, q, k_new, v_new, cache_k, cache_v, page_table, heads):
    db, t_new, w = q.shape
    n_phys, _, page = cache_k.shape
    n_pages = page_table.shape[1]
    assert KEY_BLOCK % page == 0 and t_new <= LANES
    per_block = KEY_BLOCK // page
    n_group = 16 if n_pages % 16 == 0 else per_block
    assert n_pages % n_group == 0 and n_group % per_block == 0
    n_steps = n_pages // n_group
    ht = heads * t_new
    bias_rows = jnp.repeat(bias.astype(F32), t_new).reshape(ht, 1)
    u = _suffix_matrix(KEY_BLOCK)
    tok = pl.BlockSpec((1, t_new, w), lambda b, p, pt: (b, 0, 0))

    def page_spec(g):
        return pl.BlockSpec((1, w, page),
                            lambda b, p, pt: (pt[b, (n_steps - 1 - p) * n_group + g], 0, 0))

    const = lambda shape: pl.BlockSpec(shape, lambda b, p, pt: (0,) * len(shape))
    grid_spec = pltpu.PrefetchScalarGridSpec(
        num_scalar_prefetch=1,
        grid=(db, n_steps),
        in_specs=[const((ht, 1)), tok, tok, tok]
                 + [page_spec(g) for g in range(n_group)] * 2 + [const(u.shape)],
        out_specs=tok,
        scratch_shapes=[pltpu.VMEM((ht, w), BF16), pltpu.VMEM((ht, 1), F32),
                        pltpu.VMEM((ht, w), F32)],
    )
    return pl.pallas_call(
        functools.partial(_sb_sample_kernel, n_group=n_group, heads=heads),
        grid_spec=grid_spec,
        out_shape=jax.ShapeDtypeStruct((db, t_new, w), F32),
        compiler_params=_params(2),
        name="sb_sample",
    )(page_table, bias_rows, q, k_new, v_new, *([cache_k] * n_group), *([cache_v] * n_group), u)


def _gla_kernel(q_ref, k_ref, lg_ref, v_ref, r_ref, s0_ref, gn_ref, ltri_ref, ones_ref, y_ref,
                sout_ref, st_ref, *, chunk, sub, heads):
    t = pl.program_id(1)
    nb, tb, kw = q_ref.shape
    gw = v_ref.shape[2]
    dk, dv = kw // heads, gw // heads
    n_sub = chunk // sub
    n_chunks = tb // chunk
    group = 8 if n_chunks % 8 == 0 else 1

    @pl.when(t == 0)
    def _init():
        for bi in range(nb):
            for h in range(heads):
                blocks = [s0_ref[bi, h] if g == h else jnp.zeros((dk, dv), F32)
                          for g in range(heads)]
                st_ref[bi, h * dk:(h + 1) * dk, :] = jnp.concatenate(blocks, axis=1)

    own_state = (lax.broadcasted_iota(jnp.int32, (kw, gw), 0) // dk
                 == lax.broadcasted_iota(jnp.int32, (kw, gw), 1) // dv)
    ones = ones_ref[...]
    lane_head = lax.broadcasted_iota(jnp.int32, (sub, kw), 1) // dk
    srow = lax.broadcasted_iota(jnp.int32, (chunk, kw), 0)
    att_col = lax.broadcasted_iota(jnp.int32, (heads * sub, chunk), 1)
    att_t = lax.broadcasted_iota(jnp.int32, (heads * sub, chunk), 0) % sub
    ltri = ltri_ref[...]
    gn = gn_ref[...]

    def state_free(rows_list):
        n = range(len(rows_list))
        q = [q_ref[bi, r, :] for bi, r in rows_list]
        k = [k_ref[bi, r, :] for bi, r in rows_list]
        v = [v_ref[bi, r, :] for bi, r in rows_list]
        lg = [lg_ref[bi, r, :] for bi, r in rows_list]
        lg_hi = [x.astype(BF16) for x in lg]
        lg_lo = [(x - h.astype(F32)).astype(BF16) for x, h in zip(lg, lg_hi)]
        bc = [_dot(ltri, h) + _dot(ltri, l) for h, l in zip(lg_hi, lg_lo)]
        b_last = [x[chunk - 1:chunk, :] for x in bc]
        k_last = [(k[j] * jnp.exp(b_last[j] - bc[j])).astype(BF16) for j in n]
        update = [jnp.where(own_state, _dot_tn(k_last[j], v[j]), 0.0) for j in n]
        total = [_dot_tn(lg_hi[j], ones) + _dot_tn(lg_lo[j], ones) for j in n]
        decay = [jnp.concatenate([jnp.exp(x)] * (gw // LANES), axis=1) for x in total]
        q_decayed = [(q[j] * jnp.exp(bc[j])).astype(BF16) for j in n]
        intra = [[] for _ in n]
        for sc in range(n_sub):
            lo, hi = sc * sub, (sc + 1) * sub
            seen = srow < hi
            att = []
            for j in n:
                ref = bc[j][lo - 1:lo, :] if sc > 0 else jnp.zeros((1, kw), F32)
                qd = q[j][lo:hi] * jnp.exp(bc[j][lo:hi] - ref)
                kd = (jnp.where(seen, k[j], 0.0)
                      * jnp.exp(jnp.where(seen, ref - bc[j], 0.0))).astype(BF16)
                qs = jnp.concatenate([jnp.where(lane_head == h, qd, 0.0) for h in range(heads)],
                                     axis=0).astype(BF16)
                att.append(_dot_nt(qs, kd))
            for j in n:
                a = jnp.where(att_col <= lo + att_t, att[j], 0.0).astype(BF16)
                ov = _dot(a, v[j])
                intra[j].append(jnp.concatenate(
                    [ov[h * sub:(h + 1) * sub, h * dv:(h + 1) * dv] for h in range(heads)],
                    axis=1))
        return [(q_decayed[j], jnp.concatenate(intra[j], axis=0), decay[j], update[j]) for j in n]

    def finish(rows_list, outs):
        heads_of = [[o[:, h * dv:(h + 1) * dv] for h in range(heads)] for o in outs]
        scale = [[lax.rsqrt(jnp.mean(oh * oh, axis=-1, keepdims=True) + RMS_EPS) for oh in ohs]
                 for ohs in heads_of]
        for (bi, rows), ohs, scs in zip(rows_list, heads_of, scale):
            r = r_ref[bi, rows, :]
            normed = jnp.concatenate([oh * sc for oh, sc in zip(ohs, scs)], axis=1)
            y_ref[bi, rows, :] = (normed * gn * (r * jax.nn.sigmoid(r))).astype(BF16)

    def group_body(gi, carry):
        rows = [(bi, pl.ds(pl.multiple_of((gi * group + j) * chunk, chunk), chunk))
                for j in range(group) for bi in range(nb)]
        parts = state_free(rows)
        st = [st_ref[bi] for bi in range(nb)]
        outs = []
        for (bi, _), (q_decayed, intra, decay, update) in zip(rows, parts):
            outs.append(_dot(q_decayed, st[bi].astype(BF16)) + intra)
            st[bi] = st[bi] * decay + update
        for bi in range(nb):
            st_ref[bi] = st[bi]
        finish(rows, outs)
        return carry

    lax.fori_loop(0, tb // (chunk * group), group_body, 0)

    @pl.when(t == pl.num_programs(1) - 1)
    def _final():
        for bi in range(nb):
            for h in range(heads):
                sout_ref[bi, h] = st_ref[bi, h * dk:(h + 1) * dk, h * dv:(h + 1) * dv]


def _gla(q, k, lg, v, r, s0, g_norm, heads, chunk, tb, nb=1):
    b, t, kw = q.shape
    gw = v.shape[2]
    dk, dv = kw // heads, gw // heads
    sub = min(GLA_SUB, chunk)
    tb = min(tb, t)
    assert t % tb == 0 and tb % chunk == 0 and chunk % sub == 0 and b % nb == 0
    shared_state = s0.shape[0] == 1
    assert not (shared_state and nb > 1)
    ltri = (lax.broadcasted_iota(jnp.int32, (chunk, chunk), 0)
            >= lax.broadcasted_iota(jnp.int32, (chunk, chunk), 1)).astype(BF16)
    gn = jnp.tile(g_norm.astype(F32), heads).reshape(1, gw)
    tok = lambda n: pl.BlockSpec((nb, tb, n), lambda bi, ti: (bi, ti, 0))
    state_in = pl.BlockSpec((nb, heads, dk, dv),
                            lambda bi, ti: (0 if shared_state else bi, 0, 0, 0))
    state_out = pl.BlockSpec((nb, heads, dk, dv), lambda bi, ti: (bi, 0, 0, 0))
    ones = jnp.ones((chunk, LANES), BF16)
    const = lambda shape: pl.BlockSpec(shape, lambda bi, ti: (0,) * len(shape))
    return pl.pallas_call(
        functools.partial(_gla_kernel, chunk=chunk, sub=sub, heads=heads),
        grid=(b // nb, t // tb),
        in_specs=[tok(kw), tok(kw), tok(kw), tok(gw), tok(gw), state_in, const((1, gw)),
                  const((chunk, chunk)), const((chunk, LANES))],
        out_specs=[tok(gw), state_out],
        out_shape=[jax.ShapeDtypeStruct((b, t, gw), BF16),
                   jax.ShapeDtypeStruct((b, heads, dk, dv), F32)],
        scratch_shapes=[pltpu.VMEM((nb, kw, gw), F32)],
        compiler_params=_params(2),
        name="gla",
    )(q, k, lg, v, r, s0, gn, ltri, ones)


def kernel(x_prompt, x_sample, cache_k, cache_v, state_gla, page_table, meta_tokens, ln_g, ln_b,
           w_ffn1_in, w_ffn1_out, w_mix_in, w_gate_up, b_gate, b_sb, g_gla_norm, w_out,
           w_ffn2_in, w_ffn2_out):
    depth = w_out.shape[0]
    assert depth == 1
    alpha = (2.0 * depth) ** 0.25
    b, t, d = x_prompt.shape
    db, ts, _ = x_sample.shape
    _, n_phys, page, heads, dh = cache_k.shape
    w = heads * dh
    _, _, gheads, dk, dv = state_gla.shape
    kw, gw = gheads * dk, gheads * dv
    n_meta = meta_tokens.shape[0]
    lyr = 0

    ffn1 = _prep_ffn(w_ffn1_in[lyr], w_ffn1_out[lyr])
    ffn2 = _prep_ffn(w_ffn2_in[lyr], w_ffn2_out[lyr])
    mix_w = _prep_mix(w_mix_in[lyr], w_gate_up[lyr], b_gate[lyr], w, kw, gw)
    wo_sb = w_out[lyr][:w].astype(BF16)
    wo_gla = w_out[lyr][w:].astype(BF16)
    g, bb = ln_g[lyr], ln_b[lyr]
    bias = b_sb[lyr].astype(F32)
    mix = functools.partial(_mix, mix_w=mix_w, w=w, kw=kw, gw=gw, sb_scale=dh ** -0.5 * LOG2E,
                            gla_scale=dk ** -0.5)

    hm = _ffn_ln(meta_tokens.astype(x_prompt.dtype), ffn1, g[0], bb[0], alpha)
    hp = _ffn_ln(x_prompt.reshape(b * t, d), ffn1, g[0], bb[0], alpha)
    hs = _ffn_ln(x_sample.reshape(db * ts, d), ffn1, g[0], bb[0], alpha)
    _, kmf, vmf, kmb, vmb, gqm, gkm, gvm, grm, lgm = mix(hm)
    qp, kpf, vpf, kpb, vpb, gqp, gkp, gvp, grp, lgp = mix(hp)
    qs, ksf, vsf, _, _, gqs, gks, gvs, grs, lgs = mix(hs)

    r3 = lambda a, n: a.reshape(n, a.shape[0] // n, a.shape[1])
    sb_p = _sb_prompt(bias, r3(qp, b), r3(kpb, b), r3(vpb, b), kmb, vmb, dh)
    zero_state = jnp.zeros((1, gheads, dk, dv), F32)
    _, st_meta = _gla(r3(gqm, 1), r3(gkm, 1), r3(lgm, 1), r3(gvm, 1), r3(grm, 1), zero_state,
                      g_gla_norm[lyr], gheads, n_meta, n_meta)
    gla_p, st_p = _gla(r3(gqp, b), r3(gkp, b), r3(lgp, b), r3(gvp, b), r3(grp, b), st_meta,
                       g_gla_norm[lyr], gheads, GLA_CHUNK, 512)
    y_prompt = _outproj_ffn_ln(hp, sb_p.reshape(b * t, w), gla_p.reshape(b * t, gw), wo_sb,
                               wo_gla, g[1], bb[1], ffn2, g[2], bb[2], alpha).reshape(b, t, d)

    feature_major = lambda c: c.transpose(0, 2, 3, 1).reshape(n_phys, w, page)
    sb_s = _sb_sample(bias, r3(qs, db).astype(F32), r3(ksf, db), r3(vsf, db),
                      feature_major(cache_k[lyr]), feature_major(cache_v[lyr]), page_table, heads)
    ts_pad = -(-ts // GLA_SUB) * GLA_SUB
    padt = lambda a: jnp.pad(r3(a, db), ((0, 0), (0, ts_pad - ts), (0, 0)))
    gla_s, st_s = _gla(padt(gqs), padt(gks), padt(lgs), padt(gvs), padt(grs),
                       state_gla[lyr].astype(F32), g_gla_norm[lyr], gheads, ts_pad, ts_pad,
                       nb=8 if db % 8 == 0 else 1)
    y_sample = _outproj_ffn_ln(hs, sb_s.reshape(db * ts, w).astype(BF16),
                               gla_s[:, :ts].reshape(db * ts, gw), wo_sb, wo_gla,
                               g[1], bb[1], ffn2, g[2], bb[2], alpha).reshape(db, ts, d)

    def with_meta(meta_rows, rows):
        full = jnp.concatenate([jnp.broadcast_to(meta_rows[None], (b, n_meta, w)), r3(rows, b)],
                               axis=1)
        return full.reshape(1, b, n_meta + t, heads, dh)

    sdt = state_gla.dtype
    return (y_prompt, y_sample, with_meta(kmf, kpf), with_meta(vmf, vpf),
            st_p.astype(sdt)[None],
            ksf.reshape(1, db, ts, heads, dh), vsf.reshape(1, db, ts, heads, dh),
            st_s.astype(sdt)[None])
```

```python
import functools

import jax
import jax.numpy as jnp
from jax import lax
from jax.experimental import pallas as pl
from jax.experimental.pallas import tpu as pltpu

F32 = jnp.float32
BF16 = jnp.bfloat16

LN_EPS = 1e-5
RMS_EPS = 1e-6
GLA_TAU = 16.0
GLA_CHUNK = 64
GLA_SUB = 16
LANES = 128
KEY_BLOCK = 256
LOG2E = 1.4426950408889634
VMEM_LIMIT_BYTES = 56 * 1024 * 1024


def _dot(a, b):
    return jnp.dot(a, b, preferred_element_type=F32)


def _dot_nt(a, b):
    return lax.dot_general(a, b, (((1,), (1,)), ((), ())), preferred_element_type=F32)


def _dot_tn(a, b):
    return lax.dot_general(a, b, (((0,), (0,)), ((), ())), preferred_element_type=F32)


def _params(n_grid):
    return pltpu.CompilerParams(dimension_semantics=("arbitrary",) * n_grid,
                                vmem_limit_bytes=VMEM_LIMIT_BYTES)


def _resident(shape):
    return pl.BlockSpec(shape, lambda *_: (0,) * len(shape), pipeline_mode=pl.Buffered(1))


def _layer_norm(y, g, b):
    mu = jnp.mean(y, axis=-1, keepdims=True)
    d = y - mu
    var = jnp.mean(d * d, axis=-1, keepdims=True)
    return d * lax.rsqrt(var + LN_EPS) * g + b


def _log_sigmoid(z):
    return jnp.minimum(z, 0.0) - jnp.log(1.0 + jnp.exp(-jnp.abs(z)))


def _swiglu_ln(x, wg_ref, wu_ref, wo_ref, g, b, acc_ref, alpha):
    xb = x.astype(BF16)
    for c in range(wg_ref.shape[0]):
        gate = _dot(xb, wg_ref[c])
        up = _dot(xb, wu_ref[c])
        act = (gate * jax.nn.sigmoid(gate) * up).astype(BF16)
        part = _dot(act, wo_ref[c])
        if c == 0:
            acc_ref[...] = part
        else:
            acc_ref[...] += part
    return _layer_norm(alpha * x + 0.5 * acc_ref[...], g, b)


def _ffn_ln_kernel(x_ref, wg_ref, wu_ref, wo_ref, g_ref, b_ref, o_ref, acc_ref, *, alpha):
    o_ref[...] = _swiglu_ln(x_ref[...], wg_ref, wu_ref, wo_ref, g_ref[...], b_ref[...], acc_ref,
                            alpha)


def _outproj_ffn_ln_kernel(h_ref, sb_ref, gl_ref, w1_ref, w2_ref, g1_ref, b1_ref,
                           wg_ref, wu_ref, wo_ref, g2_ref, b2_ref, o_ref, acc_ref, *, alpha):
    mixed = _dot(sb_ref[...], w1_ref[...]) + _dot(gl_ref[...], w2_ref[...])
    x = _layer_norm(alpha * h_ref[...] + mixed, g1_ref[...], b1_ref[...])
    o_ref[...] = _swiglu_ln(x, wg_ref, wu_ref, wo_ref, g2_ref[...], b2_ref[...], acc_ref, alpha)


def _prep_ffn(w_in, w_out, tf=256):
    d, f = w_in.shape[0], w_out.shape[0]
    nc = f // tf
    assert nc * tf == f
    wg = w_in[:, :f].reshape(d, nc, tf).transpose(1, 0, 2).astype(BF16)
    wu = w_in[:, f:].reshape(d, nc, tf).transpose(1, 0, 2).astype(BF16)
    wo = w_out.reshape(nc, tf, d).astype(BF16)
    return wg, wu, wo


def _ffn_ln(x, ffn_w, g, b, alpha, tm=512):
    wg, wu, wo = ffn_w
    m, d = x.shape
    tm = min(tm, m)
    assert m % tm == 0
    row = pl.BlockSpec((tm, d), lambda i: (i, 0))
    return pl.pallas_call(
        functools.partial(_ffn_ln_kernel, alpha=alpha),
        grid=(m // tm,),
        in_specs=[row, _resident(wg.shape), _resident(wu.shape), _resident(wo.shape),
                  _resident((1, d)), _resident((1, d))],
        out_specs=row,
        out_shape=jax.ShapeDtypeStruct((m, d), F32),
        scratch_shapes=[pltpu.VMEM((tm, d), F32)],
        compiler_params=_params(1),
        name="ffn_ln",
    )(x, wg, wu, wo, g.reshape(1, d), b.reshape(1, d))


def _outproj_ffn_ln(h, sb_o, gla_y, w1, w2, g1, b1, ffn_w, g2, b2, alpha, tm=512):
    wg, wu, wo = ffn_w
    m, d = h.shape
    tm = min(tm, m)
    assert m % tm == 0
    spec = lambda n: pl.BlockSpec((tm, n), lambda i: (i, 0))
    vec = lambda a: a.reshape(1, d)
    return pl.pallas_call(
        functools.partial(_outproj_ffn_ln_kernel, alpha=alpha),
        grid=(m // tm,),
        in_specs=[spec(d), spec(sb_o.shape[1]), spec(gla_y.shape[1]), _resident(w1.shape),
                  _resident(w2.shape), _resident((1, d)), _resident((1, d)),
                  _resident(wg.shape), _resident(wu.shape), _resident(wo.shape),
                  _resident((1, d)), _resident((1, d))],
        out_specs=spec(d),
        out_shape=jax.ShapeDtypeStruct((m, d), F32),
        scratch_shapes=[pltpu.VMEM((tm, d), F32)],
        compiler_params=_params(1),
        name="outproj_ffn_ln",
    )(h, sb_o, gla_y, w1, w2, vec(g1), vec(b1), wg, wu, wo, vec(g2), vec(b2))


def _mix_kernel(x_ref, wsb_ref, wgl_ref, wlr_ref, wgu_ref, bg_ref,
                q_ref, kf_ref, vf_ref, kb_ref, vb_ref, gq_ref, gk_ref, gv_ref, gr_ref, lg_ref,
                *, sb_scale, gla_scale, feature_major):
    xb = x_ref[...].astype(BF16)
    w = q_ref.shape[1]
    kw = gq_ref.shape[1]
    gw = gv_ref.shape[1]
    q_ref[...] = (_dot(xb, wsb_ref[:, 0:w]) * sb_scale).astype(BF16)
    k = _dot(xb, wsb_ref[:, w:2 * w])
    kb_ref[...] = k.astype(BF16)
    v = _dot(xb, wsb_ref[:, 2 * w:3 * w])
    vb_ref[...] = v.astype(BF16)
    if feature_major:
        kf_ref[0] = k.T
        vf_ref[0] = v.T
    else:
        kf_ref[...] = k
        vf_ref[...] = v
    gq_ref[...] = _dot(xb, wgl_ref[:, 0:kw]) * gla_scale
    gk_ref[...] = _dot(xb, wgl_ref[:, kw:2 * kw])
    gv_ref[...] = _dot(xb, wgl_ref[:, 2 * kw:2 * kw + gw]).astype(BF16)
    gr_ref[...] = _dot(xb, wgl_ref[:, 2 * kw + gw:2 * kw + 2 * gw])
    low_rank = _dot(xb, wlr_ref[...]).astype(BF16)
    gate_logit = _dot(low_rank, wgu_ref[...]) + bg_ref[...]
    lg_ref[...] = _log_sigmoid(gate_logit) * (1.0 / GLA_TAU)


def _prep_mix(w_mix, w_gate_up, b_gate, w, kw, gw):
    rank = w_gate_up.shape[0]
    assert w_mix.shape[1] == 3 * w + 2 * kw + 2 * gw + rank and rank <= LANES
    wsb = w_mix[:, :3 * w].astype(BF16)
    wgl = w_mix[:, 3 * w:3 * w + 2 * kw + 2 * gw].astype(BF16)
    wlr = jnp.pad(w_mix[:, 3 * w + 2 * kw + 2 * gw:], ((0, 0), (0, LANES - rank))).astype(BF16)
    wgu = jnp.pad(w_gate_up, ((0, LANES - rank), (0, 0))).astype(BF16)
    return wsb, wgl, wlr, wgu, b_gate.reshape(1, kw).astype(F32)


def _mix(x, mix_w, w, kw, gw, sb_scale, gla_scale, tm=512, seqs=None):
    wsb, wgl, wlr, wgu, bg = mix_w
    m, d = x.shape
    tm = min(tm, m)
    assert m % tm == 0
    spec = lambda n: pl.BlockSpec((tm, n), lambda i: (i, 0))
    sds = lambda n, dt: jax.ShapeDtypeStruct((m, n), dt)
    kv_spec, kv_sds = spec(w), sds(w, F32)
    if seqs is not None:
        per_seq = m // (seqs * tm)
        assert per_seq * seqs * tm == m
        kv_spec = pl.BlockSpec((1, w, tm), lambda i: (i // per_seq, 0, i % per_seq))
        kv_sds = jax.ShapeDtypeStruct((seqs, w, m // seqs), F32)
    return pl.pallas_call(
        functools.partial(_mix_kernel, sb_scale=sb_scale, gla_scale=gla_scale,
                          feature_major=seqs is not None),
        grid=(m // tm,),
        in_specs=[spec(d), _resident(wsb.shape), _resident(wgl.shape), _resident(wlr.shape),
                  _resident(wgu.shape), _resident(bg.shape)],
        out_specs=[spec(w), kv_spec, kv_spec, spec(w), spec(w),
                   spec(kw), spec(kw), spec(gw), spec(gw), spec(kw)],
        out_shape=[sds(w, BF16), kv_sds, kv_sds, sds(w, BF16), sds(w, BF16),
                   sds(kw, F32), sds(kw, F32), sds(gw, BF16), sds(gw, F32), sds(kw, F32)],
        compiler_params=_params(1),
        name="mix_proj",
    )(x, wsb, wgl, wlr, wgu, bg)


def _sb_logs(z2, u, mask):
    neg_abs = lax.bitcast_convert_type(
        lax.bitcast_convert_type(z2, jnp.uint32) | jnp.uint32(0x80000000), F32)
    l2 = jnp.log(1.0 + jnp.exp2(neg_abs)) * LOG2E
    lb = jnp.minimum(z2, 0.0) - l2
    lk = lb - z2
    if mask is not None:
        lk = jnp.where(mask, lk, 0.0)
    n = z2.shape[1]
    s = _dot(lk.astype(BF16), u[:n, :n])
    return lb, lk[:, 0:1], s


def _sb_finish(logs, mask, c):
    lb, lk0, s = logs
    w = jnp.exp2(lb + s + c)
    if mask is not None:
        w = jnp.where(mask, w, 0.0)
    return w.astype(BF16), c + s[:, 0:1] + lk0


def _sb_prompt_kernel(bias_ref, q_ref, k_ref, v_ref, km_ref, vm_ref, u_ref, o_ref,
                      acc_ref, z_ref, w_ref, *, tq, n_meta, dh):
    kb = KEY_BLOCK
    hp = pl.program_id(1)
    i = pl.program_id(2)
    q = q_ref[0]
    lane = lax.broadcasted_iota(jnp.int32, q.shape, 1)
    zero = jnp.zeros_like(q)
    qs = jnp.concatenate([jnp.where(lane < dh, q, zero), jnp.where(lane >= dh, q, zero)], axis=0)
    row = lax.broadcasted_iota(jnp.int32, (2 * tq, 1), 0)
    bias = jnp.where(row < tq, bias_ref[2 * hp], bias_ref[2 * hp + 1]) * LOG2E
    qpos = i * tq + jnp.where(row < tq, row, row - tq)
    u = u_ref[...]
    last_pair = (i * tq) // (2 * kb)

    def rows(ref, pair, half):
        off = pl.multiple_of(pair * (2 * kb) + half * kb, kb)
        return ref[0, pl.ds(off, kb), :]

    def logits(pair, half):
        return _dot_nt(qs, rows(k_ref, pair, half)) + bias

    def weighted_values(pair):
        return _dot(w_ref[1], rows(v_ref, pair, 1)) + _dot(w_ref[0], rows(v_ref, pair, 0))

    nm = km_ref.shape[0]
    meta_valid = lax.broadcasted_iota(jnp.int32, (2 * tq, nm), 1) < n_meta
    logs_meta = _sb_logs(_dot_nt(qs, km_ref[...]) + bias, u, meta_valid)

    col = lax.broadcasted_iota(jnp.int32, (2 * tq, kb), 1)
    visible = [last_pair * (2 * kb) + half * kb + col < qpos for half in (0, 1)]
    first_next = jnp.maximum(last_pair - 1, 0)
    logs_hi = _sb_logs(logits(last_pair, 1), u, visible[1])
    z_ref[1] = logits(first_next, 1)
    logs_lo = _sb_logs(logits(last_pair, 0), u, visible[0])
    z_ref[0] = logits(first_next, 0)
    acc_ref[...] = jnp.zeros_like(acc_ref)
    w_ref[1], c = _sb_finish(logs_hi, visible[1], jnp.zeros((2 * tq, 1), F32))
    w_ref[0], c = _sb_finish(logs_lo, visible[0], c)

    def body(n, c):
        nxt = jnp.maximum(last_pair - 2 - n, 0)
        logs_hi = _sb_logs(z_ref[1], u, None)
        pv = weighted_values(last_pair - n)
        z_ref[1] = logits(nxt, 1)
        logs_lo = _sb_logs(z_ref[0], u, None)
        z_ref[0] = logits(nxt, 0)
        acc_ref[...] += pv
        w_ref[1], c = _sb_finish(logs_hi, None, c)
        w_ref[0], c = _sb_finish(logs_lo, None, c)
        return c

    c = lax.fori_loop(0, last_pair // 2, lambda m, c: body(2 * m + 1, body(2 * m, c)), c)
    c = lax.cond(last_pair % 2 == 1, lambda c: body(last_pair - 1, c), lambda c: c, c)
    pv = weighted_values(0)
    wm, _ = _sb_finish(logs_meta, meta_valid, c)
    acc = acc_ref[...] + pv + _dot(wm, vm_ref[...])
    o_ref[0] = jnp.where(lane < dh, acc[:tq], acc[tq:]).astype(BF16)


def _suffix_matrix(n):
    r = lax.broadcasted_iota(jnp.int32, (n, n), 0)
    c = lax.broadcasted_iota(jnp.int32, (n, n), 1)
    return (r > c).astype(BF16)


def _sb_prompt(bias, q, k, v, km, vm, dh, tq=2 * KEY_BLOCK):
    b, t, w = q.shape
    assert t % (2 * KEY_BLOCK) == 0 and (2 * KEY_BLOCK) % tq == 0
    assert w % LANES == 0 and LANES == 2 * dh
    n_meta = km.shape[0]
    pad = ((0, LANES - n_meta), (0, 0))
    km = jnp.pad(km, pad)
    vm = jnp.pad(vm, pad)
    u = _suffix_matrix(KEY_BLOCK)
    qspec = pl.BlockSpec((1, tq, LANES), lambda bi, hp, i: (bi, i, hp))
    kvspec = pl.BlockSpec((1, t, LANES), lambda bi, hp, i: (bi, 0, hp))
    mspec = pl.BlockSpec((LANES, LANES), lambda bi, hp, i: (0, hp))
    return pl.pallas_call(
        functools.partial(_sb_prompt_kernel, tq=tq, n_meta=n_meta, dh=dh),
        grid=(b, w // LANES, t // tq),
        in_specs=[pl.BlockSpec(memory_space=pltpu.SMEM), qspec, kvspec, kvspec, mspec, mspec,
                  _resident(u.shape)],
        out_specs=qspec,
        out_shape=jax.ShapeDtypeStruct((b, t, w), BF16),
        scratch_shapes=[pltpu.VMEM((2 * tq, LANES), F32),
                        pltpu.VMEM((2, 2 * tq, KEY_BLOCK), F32),
                        pltpu.VMEM((2, 2 * tq, KEY_BLOCK), BF16)],
        compiler_params=_params(3),
        name="sb_prompt",
    )(bias, q, k, v, km, vm, u)


def _sb_sample_kernel(pt_ref, bias_ref, q_ref, kn_ref, vn_ref, *rest, n_group, heads):
    kp = rest[:n_group]
    vp = rest[n_group:2 * n_group]
    u_ref, o_ref, qbd_ref, c_ref, acc_ref = rest[2 * n_group:]
    p = pl.program_id(1)
    t_new, w = q_ref.shape[1], q_ref.shape[2]
    dh = w // heads
    ht = heads * t_new
    page = kp[0].shape[2]
    head_of_row = lax.broadcasted_iota(jnp.int32, (ht, w), 0) // t_new
    head_of_lane = lax.broadcasted_iota(jnp.int32, (ht, w), 1) // dh
    own = head_of_row == head_of_lane
    u = u_ref[...]
    bias = bias_ref[...] * LOG2E

    @pl.when(p == 0)
    def _start():
        qt = jnp.concatenate([q_ref[0]] * heads, axis=0)
        qbd = jnp.where(own, qt, 0.0).astype(BF16)
        qbd_ref[...] = qbd
        fill = jnp.zeros((LANES - t_new, w), F32)
        kn = jnp.concatenate([kn_ref[0], fill], axis=0).astype(BF16)
        vn = jnp.concatenate([vn_ref[0], fill], axis=0).astype(BF16)
        col = lax.broadcasted_iota(jnp.int32, (ht, LANES), 1)
        t_of_row = lax.broadcasted_iota(jnp.int32, (ht, LANES), 0) % t_new
        visible = col < t_of_row
        wn, c_ref[...] = _sb_finish(_sb_logs(_dot_nt(qbd, kn) + bias, u, visible), visible,
                                    jnp.zeros((ht, 1), F32))
        acc_ref[...] = _dot(wn, vn)

    qbd = qbd_ref[...]
    per_block = KEY_BLOCK // page
    blocks = [range(blk * per_block, (blk + 1) * per_block)
              for blk in reversed(range(n_group // per_block))]
    logs = [_sb_logs(jnp.concatenate([_dot(qbd, kp[g][0].astype(BF16)) for g in ids], axis=1)
                     + bias, u, None) for ids in blocks]
    c = c_ref[...]
    pv = None
    for ids, blk_logs in zip(blocks, logs):
        wb, c = _sb_finish(blk_logs, None, c)
        for n, g in enumerate(ids):
            part = _dot_nt(wb[:, n * page:(n + 1) * page], vp[g][0].astype(BF16))
            pv = part if pv is None else pv + part
    acc_ref[...] += pv
    c_ref[...] = c

    @pl.when(p == pl.num_programs(1) - 1)
    def _finish():
        acc = jnp.where(own, acc_ref[...], 0.0)
        out = acc[0:t_new]
        for h in range(1, heads):
            out = out + acc[h * t_new:(h + 1) * t_new]
        o_ref[0] = out


def _sb_sample(bias, q, k_new, v_new, cache_k, cache_v, page_table, heads):
    db, t_new, w = q.shape
    n_phys, _, page = cache_k.shape
    n_pages = page_table.shape[1]
    assert KEY_BLOCK % page == 0 and t_new <= LANES
    per_block = KEY_BLOCK // page
    n_group = 32 if n_pages % 32 == 0 else per_block
    assert n_pages % n_group == 0 and n_group % per_block == 0
    n_steps = n_pages // n_group
    ht = heads * t_new
    bias_rows = jnp.repeat(bias.astype(F32), t_new).reshape(ht, 1)
    u = _suffix_matrix(KEY_BLOCK)
    tok = pl.BlockSpec((1, t_new, w), lambda b, p, pt: (b, 0, 0))

    def page_spec(g):
        return pl.BlockSpec((1, w, page),
                            lambda b, p, pt: (pt[b, (n_steps - 1 - p) * n_group + g], 0, 0))

    const = lambda shape: pl.BlockSpec(shape, lambda b, p, pt: (0,) * len(shape))
    grid_spec = pltpu.PrefetchScalarGridSpec(
        num_scalar_prefetch=1,
        grid=(db, n_steps),
        in_specs=[const((ht, 1)), tok, tok, tok]
                 + [page_spec(g) for g in range(n_group)] * 2 + [const(u.shape)],
        out_specs=tok,
        scratch_shapes=[pltpu.VMEM((ht, w), BF16), pltpu.VMEM((ht, 1), F32),
                        pltpu.VMEM((ht, w), F32)],
    )
    return pl.pallas_call(
        functools.partial(_sb_sample_kernel, n_group=n_group, heads=heads),
        grid_spec=grid_spec,
        out_shape=jax.ShapeDtypeStruct((db, t_new, w), F32),
        compiler_params=_params(2),
        name="sb_sample",
    )(page_table, bias_rows, q, k_new, v_new, *([cache_k] * n_group), *([cache_v] * n_group), u)


def _gla_kernel(q_ref, k_ref, lg_ref, v_ref, r_ref, s0_ref, gn_ref, ltri_ref, ones_ref, y_ref,
                sout_ref, st_ref, *, chunk, sub, heads):
    t = pl.program_id(1)
    nb, tb, kw = q_ref.shape
    gw = v_ref.shape[2]
    dk, dv = kw // heads, gw // heads
    n_sub = chunk // sub
    n_chunks = tb // chunk
    group = 8 if n_chunks % 8 == 0 else 1

    @pl.when(t == 0)
    def _init():
        for bi in range(nb):
            for h in range(heads):
                blocks = [s0_ref[bi, h] if g == h else jnp.zeros((dk, dv), F32)
                          for g in range(heads)]
                st_ref[bi, h * dk:(h + 1) * dk, :] = jnp.concatenate(blocks, axis=1)

    own_state = (lax.broadcasted_iota(jnp.int32, (kw, gw), 0) // dk
                 == lax.broadcasted_iota(jnp.int32, (kw, gw), 1) // dv)
    ones = ones_ref[...]
    lane_head = lax.broadcasted_iota(jnp.int32, (sub, kw), 1) // dk
    srow = lax.broadcasted_iota(jnp.int32, (chunk, kw), 0)
    att_col = lax.broadcasted_iota(jnp.int32, (heads * sub, chunk), 1)
    att_t = lax.broadcasted_iota(jnp.int32, (heads * sub, chunk), 0) % sub
    ltri = ltri_ref[...]
    gn = gn_ref[...]

    def state_free(rows_list):
        n = range(len(rows_list))
        q = [q_ref[bi, r, :] for bi, r in rows_list]
        k = [k_ref[bi, r, :] for bi, r in rows_list]
        v = [v_ref[bi, r, :] for bi, r in rows_list]
        lg = [lg_ref[bi, r, :] for bi, r in rows_list]
        lg_hi = [x.astype(BF16) for x in lg]
        lg_lo = [(x - h.astype(F32)).astype(BF16) for x, h in zip(lg, lg_hi)]
        bc = [_dot(ltri, h) + _dot(ltri, l) for h, l in zip(lg_hi, lg_lo)]
        b_last = [x[chunk - 1:chunk, :] for x in bc]
        k_last = [(k[j] * jnp.exp(b_last[j] - bc[j])).astype(BF16) for j in n]
        update = [jnp.where(own_state, _dot_tn(k_last[j], v[j]), 0.0) for j in n]
        total = [_dot_tn(lg_hi[j], ones) + _dot_tn(lg_lo[j], ones) for j in n]
        decay = [jnp.concatenate([jnp.exp(x)] * (gw // LANES), axis=1) for x in total]
        q_decayed = [(q[j] * jnp.exp(bc[j])).astype(BF16) for j in n]
        intra = [[] for _ in n]
        for sc in range(n_sub):
            lo, hi = sc * sub, (sc + 1) * sub
            seen = srow < hi
            att = []
            for j in n:
                ref = bc[j][lo - 1:lo, :] if sc > 0 else jnp.zeros((1, kw), F32)
                qd = q[j][lo:hi] * jnp.exp(bc[j][lo:hi] - ref)
                kd = (jnp.where(seen, k[j], 0.0)
                      * jnp.exp(jnp.where(seen, ref - bc[j], 0.0))).astype(BF16)
                qs = jnp.concatenate([jnp.where(lane_head == h, qd, 0.0) for h in range(heads)],
                                     axis=0).astype(BF16)
                att.append(_dot_nt(qs, kd))
            for j in n:
                a = jnp.where(att_col <= lo + att_t, att[j], 0.0).astype(BF16)
                ov = _dot(a, v[j])
                intra[j].append(jnp.concatenate(
                    [ov[h * sub:(h + 1) * sub, h * dv:(h + 1) * dv] for h in range(heads)],
                    axis=1))
        return [(q_decayed[j], jnp.concatenate(intra[j], axis=0), decay[j], update[j]) for j in n]

    def finish(rows_list, outs):
        heads_of = [[o[:, h * dv:(h + 1) * dv] for h in range(heads)] for o in outs]
        scale = [[lax.rsqrt(jnp.mean(oh * oh, axis=-1, keepdims=True) + RMS_EPS) for oh in ohs]
                 for ohs in heads_of]
        for (bi, rows), ohs, scs in zip(rows_list, heads_of, scale):
            r = r_ref[bi, rows, :]
            normed = jnp.concatenate([oh * sc for oh, sc in zip(ohs, scs)], axis=1)
            y_ref[bi, rows, :] = (normed * gn * (r * jax.nn.sigmoid(r))).astype(BF16)

    def group_body(gi, carry):
        rows = [(bi, pl.ds(pl.multiple_of((gi * group + j) * chunk, chunk), chunk))
                for j in range(group) for bi in range(nb)]
        parts = state_free(rows)
        st = [st_ref[bi] for bi in range(nb)]
        outs = []
        for (bi, _), (q_decayed, intra, decay, update) in zip(rows, parts):
            outs.append(_dot(q_decayed, st[bi].astype(BF16)) + intra)
            st[bi] = st[bi] * decay + update
        for bi in range(nb):
            st_ref[bi] = st[bi]
        finish(rows, outs)
        return carry

    lax.fori_loop(0, tb // (chunk * group), group_body, 0)

    @pl.when(t == pl.num_programs(1) - 1)
    def _final():
        for bi in range(nb):
            for h in range(heads):
                sout_ref[bi, h] = st_ref[bi, h * dk:(h + 1) * dk, h * dv:(h + 1) * dv]


def _gla(q, k, lg, v, r, s0, g_norm, heads, chunk, tb, nb=1):
    b, t, kw = q.shape
    gw = v.shape[2]
    dk, dv = kw // heads, gw // heads
    sub = min(GLA_SUB, chunk)
    tb = min(tb, t)
    assert t % tb == 0 and tb % chunk == 0 and chunk % sub == 0 and b % nb == 0
    shared_state = s0.shape[0] == 1
    assert not (shared_state and nb > 1)
    ltri = (lax.broadcasted_iota(jnp.int32, (chunk, chunk), 0)
            >= lax.broadcasted_iota(jnp.int32, (chunk, chunk), 1)).astype(BF16)
    gn = jnp.tile(g_norm.astype(F32), heads).reshape(1, gw)
    tok = lambda n: pl.BlockSpec((nb, tb, n), lambda bi, ti: (bi, ti, 0))
    state_in = pl.BlockSpec((nb, heads, dk, dv),
                            lambda bi, ti: (0 if shared_state else bi, 0, 0, 0))
    state_out = pl.BlockSpec((nb, heads, dk, dv), lambda bi, ti: (bi, 0, 0, 0))
    ones = jnp.ones((chunk, LANES), BF16)
    const = lambda shape: pl.BlockSpec(shape, lambda bi, ti: (0,) * len(shape))
    return pl.pallas_call(
        functools.partial(_gla_kernel, chunk=chunk, sub=sub, heads=heads),
        grid=(b // nb, t // tb),
        in_specs=[tok(kw), tok(kw), tok(kw), tok(gw), tok(gw), state_in, const((1, gw)),
                  const((chunk, chunk)), const((chunk, LANES))],
        out_specs=[tok(gw), state_out],
        out_shape=[jax.ShapeDtypeStruct((b, t, gw), BF16),
                   jax.ShapeDtypeStruct((b, heads, dk, dv), F32)],
        scratch_shapes=[pltpu.VMEM((nb, kw, gw), F32)],
        compiler_params=_params(2),
        name="gla",
    )(q, k, lg, v, r, s0, gn, ltri, ones)


def kernel(x_prompt, x_sample, cache_k, cache_v, state_gla, page_table, meta_tokens, ln_g, ln_b,
           w_ffn1_in, w_ffn1_out, w_mix_in, w_gate_up, b_gate, b_sb, g_gla_norm, w_out,
           w_ffn2_in, w_ffn2_out):
    depth = w_out.shape[0]
    assert depth == 1
    alpha = (2.0 * depth) ** 0.25
    b, t, d = x_prompt.shape
    db, ts, _ = x_sample.shape
    _, n_phys, page, heads, dh = cache_k.shape
    w = heads * dh
    _, _, gheads, dk, dv = state_gla.shape
    kw, gw = gheads * dk, gheads * dv
    n_meta = meta_tokens.shape[0]
    lyr = 0

    ffn1 = _prep_ffn(w_ffn1_in[lyr], w_ffn1_out[lyr])
    ffn2 = _prep_ffn(w_ffn2_in[lyr], w_ffn2_out[lyr])
    mix_w = _prep_mix(w_mix_in[lyr], w_gate_up[lyr], b_gate[lyr], w, kw, gw)
    wo_sb = w_out[lyr][:w].astype(BF16)
    wo_gla = w_out[lyr][w:].astype(BF16)
    g, bb = ln_g[lyr], ln_b[lyr]
    bias = b_sb[lyr].astype(F32)
    mix = functools.partial(_mix, mix_w=mix_w, w=w, kw=kw, gw=gw, sb_scale=dh ** -0.5 * LOG2E,
                            gla_scale=dk ** -0.5)

    hm = _ffn_ln(meta_tokens.astype(x_prompt.dtype), ffn1, g[0], bb[0], alpha)
    hp = _ffn_ln(x_prompt.reshape(b * t, d), ffn1, g[0], bb[0], alpha)
    hs = _ffn_ln(x_sample.reshape(db * ts, d), ffn1, g[0], bb[0], alpha)
    _, kmf, vmf, kmb, vmb, gqm, gkm, gvm, grm, lgm = mix(hm)
    qp, kpt, vpt, kpb, vpb, gqp, gkp, gvp, grp, lgp = mix(hp, seqs=b)
    qs, ksf, vsf, _, _, gqs, gks, gvs, grs, lgs = mix(hs)

    r3 = lambda a, n: a.reshape(n, a.shape[0] // n, a.shape[1])
    sb_p = _sb_prompt(bias, r3(qp, b), r3(kpb, b), r3(vpb, b), kmb, vmb, dh)
    zero_state = jnp.zeros((1, gheads, dk, dv), F32)
    _, st_meta = _gla(r3(gqm, 1), r3(gkm, 1), r3(lgm, 1), r3(gvm, 1), r3(grm, 1), zero_state,
                      g_gla_norm[lyr], gheads, n_meta, n_meta)
    gla_p, st_p = _gla(r3(gqp, b), r3(gkp, b), r3(lgp, b), r3(gvp, b), r3(grp, b), st_meta,
                       g_gla_norm[lyr], gheads, GLA_CHUNK, 512)
    y_prompt = _outproj_ffn_ln(hp, sb_p.reshape(b * t, w), gla_p.reshape(b * t, gw), wo_sb,
                               wo_gla, g[1], bb[1], ffn2, g[2], bb[2], alpha).reshape(b, t, d)

    feature_major = lambda c: c.transpose(0, 2, 3, 1).reshape(n_phys, w, page)
    sb_s = _sb_sample(bias, r3(qs, db).astype(F32), r3(ksf, db), r3(vsf, db),
                      feature_major(cache_k[lyr]), feature_major(cache_v[lyr]), page_table, heads)
    ts_pad = -(-ts // GLA_SUB) * GLA_SUB
    padt = lambda a: jnp.pad(r3(a, db), ((0, 0), (0, ts_pad - ts), (0, 0)))
    gla_s, st_s = _gla(padt(gqs), padt(gks), padt(lgs), padt(gvs), padt(grs),
                       state_gla[lyr].astype(F32), g_gla_norm[lyr], gheads, ts_pad, ts_pad,
                       nb=8 if db % 8 == 0 else 1)
    y_sample = _outproj_ffn_ln(hs, sb_s.reshape(db * ts, w).astype(BF16),
                               gla_s[:, :ts].reshape(db * ts, gw), wo_sb, wo_gla,
                               g[1], bb[1], ffn2, g[2], bb[2], alpha).reshape(db, ts, d)

    def with_meta(meta_rows, feature_major_rows):
        full = jnp.concatenate(
            [jnp.broadcast_to(meta_rows.T[None], (b, w, n_meta)), feature_major_rows], axis=2)
        return full.reshape(1, b, heads, dh, n_meta + t).transpose(0, 1, 4, 2, 3)

    sdt = state_gla.dtype
    return (y_prompt, y_sample, with_meta(kmf, kpt), with_meta(vmf, vpt),
            st_p.astype(sdt)[None],
            ksf.reshape(1, db, ts, heads, dh), vsf.reshape(1, db, ts, heads, dh),
            st_s.astype(sdt)[None])
```

```python
import functools

import jax
import jax.numpy as jnp
from jax import lax
from jax.experimental import pallas as pl
from jax.experimental.pallas import tpu as pltpu

F32 = jnp.float32
BF16 = jnp.bfloat16

LN_EPS = 1e-5
RMS_EPS = 1e-6
GLA_TAU = 16.0
GLA_CHUNK = 64
GLA_SUB = 16
LANES = 128
KEY_BLOCK = 256
LOG2E = 1.4426950408889634
VMEM_LIMIT_BYTES = 56 * 1024 * 1024


def _dot(a, b):
    return jnp.dot(a, b, preferred_element_type=F32)


def _dot_nt(a, b):
    return lax.dot_general(a, b, (((1,), (1,)), ((), ())), preferred_element_type=F32)


def _dot_tn(a, b):
    return lax.dot_general(a, b, (((0,), (0,)), ((), ())), preferred_element_type=F32)


def _params(n_grid):
    return pltpu.CompilerParams(dimension_semantics=("arbitrary",) * n_grid,
                                vmem_limit_bytes=VMEM_LIMIT_BYTES)


def _resident(shape):
    return pl.BlockSpec(shape, lambda *_: (0,) * len(shape), pipeline_mode=pl.Buffered(1))


def _layer_norm(y, g, b):
    mu = jnp.mean(y, axis=-1, keepdims=True)
    d = y - mu
    var = jnp.mean(d * d, axis=-1, keepdims=True)
    return d * lax.rsqrt(var + LN_EPS) * g + b


def _log_sigmoid(z):
    return jnp.minimum(z, 0.0) - jnp.log(1.0 + jnp.exp(-jnp.abs(z)))


def _swiglu_ln(x, wg_ref, wu_ref, wo_ref, g, b, acc_ref, alpha):
    xb = x.astype(BF16)
    for c in range(wg_ref.shape[0]):
        gate = _dot(xb, wg_ref[c])
        up = _dot(xb, wu_ref[c])
        act = (gate * jax.nn.sigmoid(gate) * up).astype(BF16)
        part = _dot(act, wo_ref[c])
        if c == 0:
            acc_ref[...] = part
        else:
            acc_ref[...] += part
    return _layer_norm(alpha * x + 0.5 * acc_ref[...], g, b)


def _ffn_ln_kernel(x_ref, wg_ref, wu_ref, wo_ref, g_ref, b_ref, o_ref, acc_ref, *, alpha):
    o_ref[...] = _swiglu_ln(x_ref[...], wg_ref, wu_ref, wo_ref, g_ref[...], b_ref[...], acc_ref,
                            alpha)


def _outproj_ffn_ln_kernel(h_ref, sb_ref, gl_ref, w1_ref, w2_ref, g1_ref, b1_ref,
                           wg_ref, wu_ref, wo_ref, g2_ref, b2_ref, o_ref, acc_ref, *, alpha):
    mixed = _dot(sb_ref[...], w1_ref[...]) + _dot(gl_ref[...], w2_ref[...])
    x = _layer_norm(alpha * h_ref[...] + mixed, g1_ref[...], b1_ref[...])
    o_ref[...] = _swiglu_ln(x, wg_ref, wu_ref, wo_ref, g2_ref[...], b2_ref[...], acc_ref, alpha)


def _prep_ffn(w_in, w_out, tf=256):
    d, f = w_in.shape[0], w_out.shape[0]
    nc = f // tf
    assert nc * tf == f
    wg = w_in[:, :f].reshape(d, nc, tf).transpose(1, 0, 2).astype(BF16)
    wu = w_in[:, f:].reshape(d, nc, tf).transpose(1, 0, 2).astype(BF16)
    wo = w_out.reshape(nc, tf, d).astype(BF16)
    return wg, wu, wo


def _ffn_ln(x, ffn_w, g, b, alpha, tm=512):
    wg, wu, wo = ffn_w
    m, d = x.shape
    tm = min(tm, m)
    assert m % tm == 0
    row = pl.BlockSpec((tm, d), lambda i: (i, 0))
    return pl.pallas_call(
        functools.partial(_ffn_ln_kernel, alpha=alpha),
        grid=(m // tm,),
        in_specs=[row, _resident(wg.shape), _resident(wu.shape), _resident(wo.shape),
                  _resident((1, d)), _resident((1, d))],
        out_specs=row,
        out_shape=jax.ShapeDtypeStruct((m, d), F32),
        scratch_shapes=[pltpu.VMEM((tm, d), F32)],
        compiler_params=_params(1),
        name="ffn_ln",
    )(x, wg, wu, wo, g.reshape(1, d), b.reshape(1, d))


def _outproj_ffn_ln(h, sb_o, gla_y, w1, w2, g1, b1, ffn_w, g2, b2, alpha, tm=512):
    wg, wu, wo = ffn_w
    m, d = h.shape
    tm = min(tm, m)
    assert m % tm == 0
    spec = lambda n: pl.BlockSpec((tm, n), lambda i: (i, 0))
    vec = lambda a: a.reshape(1, d)
    return pl.pallas_call(
        functools.partial(_outproj_ffn_ln_kernel, alpha=alpha),
        grid=(m // tm,),
        in_specs=[spec(d), spec(sb_o.shape[1]), spec(gla_y.shape[1]), _resident(w1.shape),
                  _resident(w2.shape), _resident((1, d)), _resident((1, d)),
                  _resident(wg.shape), _resident(wu.shape), _resident(wo.shape),
                  _resident((1, d)), _resident((1, d))],
        out_specs=spec(d),
        out_shape=jax.ShapeDtypeStruct((m, d), F32),
        scratch_shapes=[pltpu.VMEM((tm, d), F32)],
        compiler_params=_params(1),
        name="outproj_ffn_ln",
    )(h, sb_o, gla_y, w1, w2, vec(g1), vec(b1), wg, wu, wo, vec(g2), vec(b2))


def _mix_kernel(x_ref, wsb_ref, wgl_ref, wlr_ref, wgu_ref, bg_ref,
                q_ref, kf_ref, vf_ref, kb_ref, vb_ref, gq_ref, gk_ref, gv_ref, gr_ref, lg_ref,
                *, sb_scale, gla_scale, feature_major):
    xb = x_ref[...].astype(BF16)
    w = q_ref.shape[1]
    kw = gq_ref.shape[1]
    gw = gv_ref.shape[1]
    q_ref[...] = (_dot(xb, wsb_ref[:, 0:w]) * sb_scale).astype(BF16)
    k = _dot(xb, wsb_ref[:, w:2 * w])
    kb_ref[...] = k.astype(BF16)
    v = _dot(xb, wsb_ref[:, 2 * w:3 * w])
    vb_ref[...] = v.astype(BF16)
    if feature_major:
        kf_ref[0] = k.T
        vf_ref[0] = v.T
    else:
        kf_ref[...] = k
        vf_ref[...] = v
    gq_ref[...] = _dot(xb, wgl_ref[:, 0:kw]) * gla_scale
    gk_ref[...] = _dot(xb, wgl_ref[:, kw:2 * kw])
    gv_ref[...] = _dot(xb, wgl_ref[:, 2 * kw:2 * kw + gw]).astype(BF16)
    gr_ref[...] = _dot(xb, wgl_ref[:, 2 * kw + gw:2 * kw + 2 * gw])
    low_rank = _dot(xb, wlr_ref[...]).astype(BF16)
    gate_logit = _dot(low_rank, wgu_ref[...]) + bg_ref[...]
    lg_ref[...] = _log_sigmoid(gate_logit) * (1.0 / GLA_TAU)


def _prep_mix(w_mix, w_gate_up, b_gate, w, kw, gw):
    rank = w_gate_up.shape[0]
    assert w_mix.shape[1] == 3 * w + 2 * kw + 2 * gw + rank and rank <= LANES
    wsb = w_mix[:, :3 * w].astype(BF16)
    wgl = w_mix[:, 3 * w:3 * w + 2 * kw + 2 * gw].astype(BF16)
    wlr = jnp.pad(w_mix[:, 3 * w + 2 * kw + 2 * gw:], ((0, 0), (0, LANES - rank))).astype(BF16)
    wgu = jnp.pad(w_gate_up, ((0, LANES - rank), (0, 0))).astype(BF16)
    return wsb, wgl, wlr, wgu, b_gate.reshape(1, kw).astype(F32)


def _mix(x, mix_w, w, kw, gw, sb_scale, gla_scale, tm=512, seqs=None):
    wsb, wgl, wlr, wgu, bg = mix_w
    m, d = x.shape
    tm = min(tm, m)
    assert m % tm == 0
    spec = lambda n: pl.BlockSpec((tm, n), lambda i: (i, 0))
    sds = lambda n, dt: jax.ShapeDtypeStruct((m, n), dt)
    kv_spec, kv_sds = spec(w), sds(w, F32)
    if seqs is not None:
        per_seq = m // (seqs * tm)
        assert per_seq * seqs * tm == m
        kv_spec = pl.BlockSpec((1, w, tm), lambda i: (i // per_seq, 0, i % per_seq))
        kv_sds = jax.ShapeDtypeStruct((seqs, w, m // seqs), F32)
    return pl.pallas_call(
        functools.partial(_mix_kernel, sb_scale=sb_scale, gla_scale=gla_scale,
                          feature_major=seqs is not None),
        grid=(m // tm,),
        in_specs=[spec(d), _resident(wsb.shape), _resident(wgl.shape), _resident(wlr.shape),
                  _resident(wgu.shape), _resident(bg.shape)],
        out_specs=[spec(w), kv_spec, kv_spec, spec(w), spec(w),
                   spec(kw), spec(kw), spec(gw), spec(gw), spec(kw)],
        out_shape=[sds(w, BF16), kv_sds, kv_sds, sds(w, BF16), sds(w, BF16),
                   sds(kw, F32), sds(kw, F32), sds(gw, BF16), sds(gw, F32), sds(kw, F32)],
        compiler_params=_params(1),
        name="mix_proj",
    )(x, wsb, wgl, wlr, wgu, bg)


def _sb_logs(z2, u, mask):
    neg_abs = lax.bitcast_convert_type(
        lax.bitcast_convert_type(z2, jnp.uint32) | jnp.uint32(0x80000000), F32)
    l2 = jnp.log(1.0 + jnp.exp2(neg_abs)) * LOG2E
    lb = jnp.minimum(z2, 0.0) - l2
    lk = lb - z2
    if mask is not None:
        lk = jnp.where(mask, lk, 0.0)
    n = z2.shape[1]
    s = _dot(lk.astype(BF16), u[:n, :n])
    return lb, lk[:, 0:1], s


def _sb_finish(logs, mask, c):
    lb, lk0, s = logs
    w = jnp.exp2(lb + s + c)
    if mask is not None:
        w = jnp.where(mask, w, 0.0)
    return w.astype(BF16), c + s[:, 0:1] + lk0


def _sb_prompt_kernel(bias_ref, q_ref, k_ref, v_ref, km_ref, vm_ref, u_ref, o_ref,
                      acc_ref, z_ref, w_ref, *, tq, n_meta, dh):
    kb = KEY_BLOCK
    hp = pl.program_id(1)
    i = pl.program_id(2)
    q = q_ref[0]
    lane = lax.broadcasted_iota(jnp.int32, q.shape, 1)
    zero = jnp.zeros_like(q)
    q0 = jnp.where(lane < dh, q, zero)
    q1 = jnp.where(lane >= dh, q, zero)
    qs = jnp.concatenate([q0[:kb], q1[:kb], q0[kb:], q1[kb:]], axis=0)
    row = lax.broadcasted_iota(jnp.int32, (2 * tq, 1), 0)
    second_head = ((row >= kb) & (row < 2 * kb)) | (row >= 3 * kb)
    token = jnp.where(row < kb, row, jnp.where(row < 3 * kb, row - kb, row - 2 * kb))
    bias = jnp.where(second_head, bias_ref[2 * hp + 1], bias_ref[2 * hp]) * LOG2E
    bias_hi = lax.bitcast_convert_type(
        lax.bitcast_convert_type(bias, jnp.uint32) & jnp.uint32(0xFFFF0000), F32)
    lane2 = lax.broadcasted_iota(jnp.int32, (2 * tq, LANES), 1)
    bias_lanes = (jnp.where(lane2 == 0, 1.0, 0.0) * bias_hi
                  + jnp.where(lane2 == 1, 1.0, 0.0) * (bias - bias_hi))
    qs = jnp.concatenate([qs, bias_lanes.astype(BF16)], axis=1)
    key_ones = jnp.ones((kb, LANES), BF16)
    qpos = i * tq + token
    u = u_ref[...]
    last_pair = i

    def rows(ref, pair, half):
        off = pl.multiple_of(pair * (2 * kb) + half * kb, kb)
        return ref[0, pl.ds(off, kb), :]

    def logits(pair, half, queries=qs):
        return _dot_nt(queries, jnp.concatenate([rows(k_ref, pair, half), key_ones], axis=1))

    def weighted_values(pair):
        return _dot(w_ref[1], rows(v_ref, pair, 1)) + _dot(w_ref[0], rows(v_ref, pair, 0))

    nm = km_ref.shape[0]
    meta_valid = lax.broadcasted_iota(jnp.int32, (2 * tq, nm), 1) < n_meta
    logs_meta = _sb_logs(_dot_nt(qs, jnp.concatenate([km_ref[...], key_ones[:nm]], axis=1)), u,
                         meta_valid)

    col = lax.broadcasted_iota(jnp.int32, (2 * tq, kb), 1)
    visible_lo = last_pair * (2 * kb) + col < qpos
    visible_hi = (last_pair * (2 * kb) + kb + col < qpos)[tq:]
    first_next = jnp.maximum(last_pair - 1, 0)
    logs_hi = _sb_logs(logits(last_pair, 1, qs[tq:]), u, visible_hi)
    z_ref[1] = logits(first_next, 1)
    logs_lo = _sb_logs(logits(last_pair, 0), u, visible_lo)
    z_ref[0] = logits(first_next, 0)
    acc_ref[...] = jnp.zeros_like(acc_ref)
    w_hi, c_hi = _sb_finish(logs_hi, visible_hi, jnp.zeros((tq, 1), F32))
    w_ref[1, :tq] = jnp.zeros((tq, kb), BF16)
    w_ref[1, tq:] = w_hi
    w_ref[0], c = _sb_finish(logs_lo, visible_lo,
                             jnp.concatenate([jnp.zeros((tq, 1), F32), c_hi], axis=0))

    def body(n, c):
        nxt = jnp.maximum(last_pair - 2 - n, 0)
        logs_hi = _sb_logs(z_ref[1], u, None)
        pv = weighted_values(last_pair - n)
        z_ref[1] = logits(nxt, 1)
        logs_lo = _sb_logs(z_ref[0], u, None)
        z_ref[0] = logits(nxt, 0)
        acc_ref[...] += pv
        w_ref[1], c = _sb_finish(logs_hi, None, c)
        w_ref[0], c = _sb_finish(logs_lo, None, c)
        return c

    c = lax.fori_loop(0, last_pair // 2, lambda m, c: body(2 * m + 1, body(2 * m, c)), c)
    c = lax.cond(last_pair % 2 == 1, lambda c: body(last_pair - 1, c), lambda c: c, c)
    pv = weighted_values(0)
    wm, _ = _sb_finish(logs_meta, meta_valid, c)
    acc = acc_ref[...] + pv + _dot(wm, vm_ref[...])
    first_head = lax.broadcasted_iota(jnp.int32, (kb, LANES), 1) < dh
    o_ref[0, :kb] = jnp.where(first_head, acc[:kb], acc[kb:2 * kb]).astype(BF16)
    o_ref[0, kb:] = jnp.where(first_head, acc[2 * kb:3 * kb], acc[3 * kb:]).astype(BF16)


def _suffix_matrix(n):
    r = lax.broadcasted_iota(jnp.int32, (n, n), 0)
    c = lax.broadcasted_iota(jnp.int32, (n, n), 1)
    return (r > c).astype(BF16)


def _sb_prompt(bias, q, k, v, km, vm, dh, tq=2 * KEY_BLOCK):
    b, t, w = q.shape
    assert t % tq == 0 and tq == 2 * KEY_BLOCK
    assert w % LANES == 0 and LANES == 2 * dh
    n_meta = km.shape[0]
    pad = ((0, LANES - n_meta), (0, 0))
    km = jnp.pad(km, pad)
    vm = jnp.pad(vm, pad)
    u = _suffix_matrix(KEY_BLOCK)
    qspec = pl.BlockSpec((1, tq, LANES), lambda bi, hp, i: (bi, i, hp))
    kvspec = pl.BlockSpec((1, t, LANES), lambda bi, hp, i: (bi, 0, hp))
    mspec = pl.BlockSpec((LANES, LANES), lambda bi, hp, i: (0, hp))
    return pl.pallas_call(
        functools.partial(_sb_prompt_kernel, tq=tq, n_meta=n_meta, dh=dh),
        grid=(b, w // LANES, t // tq),
        in_specs=[pl.BlockSpec(memory_space=pltpu.SMEM), qspec, kvspec, kvspec, mspec, mspec,
                  _resident(u.shape)],
        out_specs=qspec,
        out_shape=jax.ShapeDtypeStruct((b, t, w), BF16),
        scratch_shapes=[pltpu.VMEM((2 * tq, LANES), F32),
                        pltpu.VMEM((2, 2 * tq, KEY_BLOCK), F32),
                        pltpu.VMEM((2, 2 * tq, KEY_BLOCK), BF16)],
        compiler_params=_params(3),
        name="sb_prompt",
    )(bias, q, k, v, km, vm, u)


def _sb_sample_kernel(pt_ref, bias_ref, q_ref, kn_ref, vn_ref, *rest, n_group, heads):
    kp = rest[:n_group]
    vp = rest[n_group:2 * n_group]
    u_ref, o_ref, qbd_ref, c_ref, acc_ref = rest[2 * n_group:]
    p = pl.program_id(1)
    t_new, w = q_ref.shape[1], q_ref.shape[2]
    dh = w // heads
    ht = heads * t_new
    page = kp[0].shape[2]
    head_of_row = lax.broadcasted_iota(jnp.int32, (ht, w), 0) // t_new
    head_of_lane = lax.broadcasted_iota(jnp.int32, (ht, w), 1) // dh
    own = head_of_row == head_of_lane
    u = u_ref[...]
    bias = bias_ref[...] * LOG2E

    @pl.when(p == 0)
    def _start():
        qt = jnp.concatenate([q_ref[0]] * heads, axis=0)
        qbd = jnp.where(own, qt, 0.0).astype(BF16)
        qbd_ref[...] = qbd
        fill = jnp.zeros((LANES - t_new, w), F32)
        kn = jnp.concatenate([kn_ref[0], fill], axis=0).astype(BF16)
        vn = jnp.concatenate([vn_ref[0], fill], axis=0).astype(BF16)
        col = lax.broadcasted_iota(jnp.int32, (ht, LANES), 1)
        t_of_row = lax.broadcasted_iota(jnp.int32, (ht, LANES), 0) % t_new
        visible = col < t_of_row
        wn, c_ref[...] = _sb_finish(_sb_logs(_dot_nt(qbd, kn) + bias, u, visible), visible,
                                    jnp.zeros((ht, 1), F32))
        acc_ref[...] = _dot(wn, vn)

    qbd = qbd_ref[...]
    per_block = KEY_BLOCK // page
    blocks = [range(blk * per_block, (blk + 1) * per_block)
              for blk in reversed(range(n_group // per_block))]
    logs = [_sb_logs(jnp.concatenate([_dot(qbd, kp[g][0].astype(BF16)) for g in ids], axis=1)
                     + bias, u, None) for ids in blocks]
    c = c_ref[...]
    pv = None
    for ids, blk_logs in zip(blocks, logs):
        wb, c = _sb_finish(blk_logs, None, c)
        for n, g in enumerate(ids):
            part = _dot_nt(wb[:, n * page:(n + 1) * page], vp[g][0].astype(BF16))
            pv = part if pv is None else pv + part
    acc_ref[...] += pv
    c_ref[...] = c

    @pl.when(p == pl.num_programs(1) - 1)
    def _finish():
        acc = jnp.where(own, acc_ref[...], 0.0)
        out = acc[0:t_new]
        for h in range(1, heads):
            out = out + acc[h * t_new:(h + 1) * t_new]
        o_ref[0] = out


def _sb_sample(bias, q, k_new, v_new, cache_k, cache_v, page_table, heads):
    db, t_new, w = q.shape
    n_phys, _, page = cache_k.shape
    n_pages = page_table.shape[1]
    assert KEY_BLOCK % page == 0 and t_new <= LANES
    per_block = KEY_BLOCK // page
    n_group = 32 if n_pages % 32 == 0 else per_block
    assert n_pages % n_group == 0 and n_group % per_block == 0
    n_steps = n_pages // n_group
    ht = heads * t_new
    bias_rows = jnp.repeat(bias.astype(F32), t_new).reshape(ht, 1)
    u = _suffix_matrix(KEY_BLOCK)
    tok = pl.BlockSpec((1, t_new, w), lambda b, p, pt: (b, 0, 0))

    def page_spec(g):
        return pl.BlockSpec((1, w, page),
                            lambda b, p, pt: (pt[b, (n_steps - 1 - p) * n_group + g], 0, 0))

    const = lambda shape: pl.BlockSpec(shape, lambda b, p, pt: (0,) * len(shape))
    grid_spec = pltpu.PrefetchScalarGridSpec(
        num_scalar_prefetch=1,
        grid=(db, n_steps),
        in_specs=[const((ht, 1)), tok, tok, tok]
                 + [page_spec(g) for g in range(n_group)] * 2 + [const(u.shape)],
        out_specs=tok,
        scratch_shapes=[pltpu.VMEM((ht, w), BF16), pltpu.VMEM((ht, 1), F32),
                        pltpu.VMEM((ht, w), F32)],
    )
    return pl.pallas_call(
        functools.partial(_sb_sample_kernel, n_group=n_group, heads=heads),
        grid_spec=grid_spec,
        out_shape=jax.ShapeDtypeStruct((db, t_new, w), F32),
        compiler_params=_params(2),
        name="sb_sample",
    )(page_table, bias_rows, q, k_new, v_new, *([cache_k] * n_group), *([cache_v] * n_group), u)


def _gla_kernel(q_ref, k_ref, lg_ref, v_ref, r_ref, s0_ref, gn_ref, ltri_ref, ones_ref, y_ref,
                sout_ref, st_ref, *, chunk, sub, heads):
    t = pl.program_id(1)
    nb, tb, kw = q_ref.shape
    gw = v_ref.shape[2]
    dk, dv = kw // heads, gw // heads
    n_sub = chunk // sub
    n_chunks = tb // chunk
    group = 8 if n_chunks % 8 == 0 else 1

    @pl.when(t == 0)
    def _init():
        for bi in range(nb):
            for h in range(heads):
                blocks = [s0_ref[bi, h] if g == h else jnp.zeros((dk, dv), F32)
                          for g in range(heads)]
                st_ref[bi, h * dk:(h + 1) * dk, :] = jnp.concatenate(blocks, axis=1)

    own_state = (lax.broadcasted_iota(jnp.int32, (kw, gw), 0) // dk
                 == lax.broadcasted_iota(jnp.int32, (kw, gw), 1) // dv)
    ones = ones_ref[...]
    lane_head = lax.broadcasted_iota(jnp.int32, (sub, kw), 1) // dk
    srow = lax.broadcasted_iota(jnp.int32, (chunk, kw), 0)
    att_col = lax.broadcasted_iota(jnp.int32, (heads * sub, chunk), 1)
    att_t = lax.broadcasted_iota(jnp.int32, (heads * sub, chunk), 0) % sub
    ltri = ltri_ref[...]
    gn = gn_ref[...]

    def state_free(rows_list):
        n = range(len(rows_list))
        q = [q_ref[bi, r, :] for bi, r in rows_list]
        k = [k_ref[bi, r, :] for bi, r in rows_list]
        v = [v_ref[bi, r, :] for bi, r in rows_list]
        lg = [lg_ref[bi, r, :] for bi, r in rows_list]
        lg_hi = [x.astype(BF16) for x in lg]
        lg_lo = [(x - h.astype(F32)).astype(BF16) for x, h in zip(lg, lg_hi)]
        bc = [_dot(ltri, h) + _dot(ltri, l) for h, l in zip(lg_hi, lg_lo)]
        b_last = [x[chunk - 1:chunk, :] for x in bc]
        k_last = [(k[j] * jnp.exp(b_last[j] - bc[j])).astype(BF16) for j in n]
        update = [jnp.where(own_state, _dot_tn(k_last[j], v[j]), 0.0) for j in n]
        total = [_dot_tn(lg_hi[j], ones) + _dot_tn(lg_lo[j], ones) for j in n]
        decay = [jnp.concatenate([jnp.exp(x)] * (gw // LANES), axis=1) for x in total]
        q_decayed = [(q[j] * jnp.exp(bc[j])).astype(BF16) for j in n]
        intra = [[] for _ in n]
        for sc in range(n_sub):
            lo, hi = sc * sub, (sc + 1) * sub
            seen = srow < hi
            att = []
            for j in n:
                ref = bc[j][lo - 1:lo, :] if sc > 0 else jnp.zeros((1, kw), F32)
                qd = q[j][lo:hi] * jnp.exp(bc[j][lo:hi] - ref)
                kd = (jnp.where(seen, k[j], 0.0)
                      * jnp.exp(jnp.where(seen, ref - bc[j], 0.0))).astype(BF16)
                qs = jnp.concatenate([jnp.where(lane_head == h, qd, 0.0) for h in range(heads)],
                                     axis=0).astype(BF16)
                att.append(_dot_nt(qs, kd))
            for j in n:
                a = jnp.where(att_col <= lo + att_t, att[j], 0.0).astype(BF16)
                ov = _dot(a, v[j])
                intra[j].append(jnp.concatenate(
                    [ov[h * sub:(h + 1) * sub, h * dv:(h + 1) * dv] for h in range(heads)],
                    axis=1))
        return [(q_decayed[j], jnp.concatenate(intra[j], axis=0), decay[j], update[j]) for j in n]

    def finish(rows_list, outs):
        heads_of = [[o[:, h * dv:(h + 1) * dv] for h in range(heads)] for o in outs]
        scale = [[lax.rsqrt(jnp.mean(oh * oh, axis=-1, keepdims=True) + RMS_EPS) for oh in ohs]
                 for ohs in heads_of]
        for (bi, rows), ohs, scs in zip(rows_list, heads_of, scale):
            r = r_ref[bi, rows, :]
            normed = jnp.concatenate([oh * sc for oh, sc in zip(ohs, scs)], axis=1)
            y_ref[bi, rows, :] = (normed * gn * (r * jax.nn.sigmoid(r))).astype(BF16)

    def group_body(gi, carry):
        rows = [(bi, pl.ds(pl.multiple_of((gi * group + j) * chunk, chunk), chunk))
                for j in range(group) for bi in range(nb)]
        parts = state_free(rows)
        st = [st_ref[bi] for bi in range(nb)]
        outs = []
        for (bi, _), (q_decayed, intra, decay, update) in zip(rows, parts):
            outs.append(_dot(q_decayed, st[bi].astype(BF16)) + intra)
            st[bi] = st[bi] * decay + update
        for bi in range(nb):
            st_ref[bi] = st[bi]
        finish(rows, outs)
        return carry

    lax.fori_loop(0, tb // (chunk * group), group_body, 0)

    @pl.when(t == pl.num_programs(1) - 1)
    def _final():
        for bi in range(nb):
            for h in range(heads):
                sout_ref[bi, h] = st_ref[bi, h * dk:(h + 1) * dk, h * dv:(h + 1) * dv]


def _gla(q, k, lg, v, r, s0, g_norm, heads, chunk, tb, nb=1):
    b, t, kw = q.shape
    gw = v.shape[2]
    dk, dv = kw // heads, gw // heads
    sub = min(GLA_SUB, chunk)
    tb = min(tb, t)
    assert t % tb == 0 and tb % chunk == 0 and chunk % sub == 0 and b % nb == 0
    shared_state = s0.shape[0] == 1
    assert not (shared_state and nb > 1)
    ltri = (lax.broadcasted_iota(jnp.int32, (chunk, chunk), 0)
            >= lax.broadcasted_iota(jnp.int32, (chunk, chunk), 1)).astype(BF16)
    gn = jnp.tile(g_norm.astype(F32), heads).reshape(1, gw)
    tok = lambda n: pl.BlockSpec((nb, tb, n), lambda bi, ti: (bi, ti, 0))
    state_in = pl.BlockSpec((nb, heads, dk, dv),
                            lambda bi, ti: (0 if shared_state else bi, 0, 0, 0))
    state_out = pl.BlockSpec((nb, heads, dk, dv), lambda bi, ti: (bi, 0, 0, 0))
    ones = jnp.ones((chunk, LANES), BF16)
    const = lambda shape: pl.BlockSpec(shape, lambda bi, ti: (0,) * len(shape))
    return pl.pallas_call(
        functools.partial(_gla_kernel, chunk=chunk, sub=sub, heads=heads),
        grid=(b // nb, t // tb),
        in_specs=[tok(kw), tok(kw), tok(kw), tok(gw), tok(gw), state_in, const((1, gw)),
                  const((chunk, chunk)), const((chunk, LANES))],
        out_specs=[tok(gw), state_out],
        out_shape=[jax.ShapeDtypeStruct((b, t, gw), BF16),
                   jax.ShapeDtypeStruct((b, heads, dk, dv), F32)],
        scratch_shapes=[pltpu.VMEM((nb, kw, gw), F32)],
        compiler_params=_params(2),
        name="gla",
    )(q, k, lg, v, r, s0, gn, ltri, ones)


def kernel(x_prompt, x_sample, cache_k, cache_v, state_gla, page_table, meta_tokens, ln_g, ln_b,
           w_ffn1_in, w_ffn1_out, w_mix_in, w_gate_up, b_gate, b_sb, g_gla_norm, w_out,
           w_ffn2_in, w_ffn2_out):
    depth = w_out.shape[0]
    assert depth == 1
    alpha = (2.0 * depth) ** 0.25
    b, t, d = x_prompt.shape
    db, ts, _ = x_sample.shape
    _, n_phys, page, heads, dh = cache_k.shape
    w = heads * dh
    _, _, gheads, dk, dv = state_gla.shape
    kw, gw = gheads * dk, gheads * dv
    n_meta = meta_tokens.shape[0]
    lyr = 0

    ffn1 = _prep_ffn(w_ffn1_in[lyr], w_ffn1_out[lyr])
    ffn2 = _prep_ffn(w_ffn2_in[lyr], w_ffn2_out[lyr])
    mix_w = _prep_mix(w_mix_in[lyr], w_gate_up[lyr], b_gate[lyr], w, kw, gw)
    wo_sb = w_out[lyr][:w].astype(BF16)
    wo_gla = w_out[lyr][w:].astype(BF16)
    g, bb = ln_g[lyr], ln_b[lyr]
    bias = b_sb[lyr].astype(F32)
    mix = functools.partial(_mix, mix_w=mix_w, w=w, kw=kw, gw=gw, sb_scale=dh ** -0.5 * LOG2E,
                            gla_scale=dk ** -0.5)

    hm = _ffn_ln(meta_tokens.astype(x_prompt.dtype), ffn1, g[0], bb[0], alpha)
    hp = _ffn_ln(x_prompt.reshape(b * t, d), ffn1, g[0], bb[0], alpha)
    hs = _ffn_ln(x_sample.reshape(db * ts, d), ffn1, g[0], bb[0], alpha)
    _, kmf, vmf, kmb, vmb, gqm, gkm, gvm, grm, lgm = mix(hm)
    qp, kpt, vpt, kpb, vpb, gqp, gkp, gvp, grp, lgp = mix(hp, seqs=b)
    qs, ksf, vsf, _, _, gqs, gks, gvs, grs, lgs = mix(hs)

    r3 = lambda a, n: a.reshape(n, a.shape[0] // n, a.shape[1])
    sb_p = _sb_prompt(bias, r3(qp, b), r3(kpb, b), r3(vpb, b), kmb, vmb, dh)
    zero_state = jnp.zeros((1, gheads, dk, dv), F32)
    _, st_meta = _gla(r3(gqm, 1), r3(gkm, 1), r3(lgm, 1), r3(gvm, 1), r3(grm, 1), zero_state,
                      g_gla_norm[lyr], gheads, n_meta, n_meta)
    gla_p, st_p = _gla(r3(gqp, b), r3(gkp, b), r3(lgp, b), r3(gvp, b), r3(grp, b), st_meta,
                       g_gla_norm[lyr], gheads, GLA_CHUNK, 512)
    y_prompt = _outproj_ffn_ln(hp, sb_p.reshape(b * t, w), gla_p.reshape(b * t, gw), wo_sb,
                               wo_gla, g[1], bb[1], ffn2, g[2], bb[2], alpha).reshape(b, t, d)

    feature_major = lambda c: c.transpose(0, 2, 3, 1).reshape(n_phys, w, page)
    sb_s = _sb_sample(bias, r3(qs, db).astype(F32), r3(ksf, db), r3(vsf, db),
                      feature_major(cache_k[lyr]), feature_major(cache_v[lyr]), page_table, heads)
    ts_pad = -(-ts // GLA_SUB) * GLA_SUB
    padt = lambda a: jnp.pad(r3(a, db), ((0, 0), (0, ts_pad - ts), (0, 0)))
    gla_s, st_s = _gla(padt(gqs), padt(gks), padt(lgs), padt(gvs), padt(grs),
                       state_gla[lyr].astype(F32), g_gla_norm[lyr], gheads, ts_pad, ts_pad,
                       nb=8 if db % 8 == 0 else 1)
    y_sample = _outproj_ffn_ln(hs, sb_s.reshape(db * ts, w).astype(BF16),
                               gla_s[:, :ts].reshape(db * ts, gw), wo_sb, wo_gla,
                               g[1], bb[1], ffn2, g[2], bb[2], alpha).reshape(db, ts, d)

    def with_meta(meta_rows, feature_major_rows):
        full = jnp.concatenate(
            [jnp.broadcast_to(meta_rows.T[None], (b, w, n_meta)), feature_major_rows], axis=2)
        return full.reshape(1, b, heads, dh, n_meta + t).transpose(0, 1, 4, 2, 3)

    sdt = state_gla.dtype
    return (y_prompt, y_sample, with_meta(kmf, kpt), with_meta(vmf, vpt),
            st_p.astype(sdt)[None],
            ksf.reshape(1, db, ts, heads, dh), vsf.reshape(1, db, ts, heads, dh),
            st_s.astype(sdt)[None])
```

```python
import functools

import jax
import jax.numpy as jnp
from jax import lax
from jax.experimental import pallas as pl
from jax.experimental.pallas import tpu as pltpu

F32 = jnp.float32
BF16 = jnp.bfloat16

LN_EPS = 1e-5
RMS_EPS = 1e-6
GLA_TAU = 16.0
GLA_CHUNK = 64
GLA_SUB = 16
LANES = 128
MXU_TILE = 256
KEY_BLOCK = MXU_TILE
FFN_CHUNK = MXU_TILE
ROW_TILE = 512
PAGES_PER_STEP = 32
GLA_GROUP = 8
GLA_SEQS_PER_STEP = 8
LOG2E = 1.4426950408889634
VMEM_LIMIT_BYTES = 56 * 1024 * 1024


def _dot(a, b):
    return jnp.dot(a, b, preferred_element_type=F32)


def _dot_nt(a, b):
    return lax.dot_general(a, b, (((1,), (1,)), ((), ())), preferred_element_type=F32)


def _dot_tn(a, b):
    return lax.dot_general(a, b, (((0,), (0,)), ((), ())), preferred_element_type=F32)


def _params(n_grid):
    return pltpu.CompilerParams(dimension_semantics=("arbitrary",) * n_grid,
                                vmem_limit_bytes=VMEM_LIMIT_BYTES)


def _resident(shape):
    return pl.BlockSpec(shape, lambda *_: (0,) * len(shape), pipeline_mode=pl.Buffered(1))


def _layer_norm(y, g, b):
    mu = jnp.mean(y, axis=-1, keepdims=True)
    d = y - mu
    var = jnp.mean(d * d, axis=-1, keepdims=True)
    return d * lax.rsqrt(var + LN_EPS) * g + b


def _log_sigmoid(z):
    return jnp.minimum(z, 0.0) - jnp.log(1.0 + jnp.exp(-jnp.abs(z)))


def _swiglu_ln(x, win_ref, wout_ref, g, b, acc_ref, alpha):
    xb = x.astype(BF16)
    f = wout_ref.shape[0]
    for c in range(f // FFN_CHUNK):
        lo, hi = c * FFN_CHUNK, (c + 1) * FFN_CHUNK
        gate = _dot(xb, win_ref[:, lo:hi])
        up = _dot(xb, win_ref[:, f + lo:f + hi])
        act = (gate * jax.nn.sigmoid(gate) * up).astype(BF16)
        part = _dot(act, wout_ref[lo:hi, :])
        if c == 0:
            acc_ref[...] = part
        else:
            acc_ref[...] += part
    return _layer_norm(alpha * x + 0.5 * acc_ref[...], g, b)


def _ffn_ln_kernel(x_ref, win_ref, wout_ref, g_ref, b_ref, o_ref, acc_ref, *, alpha):
    o_ref[...] = _swiglu_ln(x_ref[...], win_ref, wout_ref, g_ref[...], b_ref[...], acc_ref, alpha)


def _outproj_ffn_ln_kernel(h_ref, sb_ref, gl_ref, w1_ref, w2_ref, g1_ref, b1_ref,
                           win_ref, wout_ref, g2_ref, b2_ref, o_ref, acc_ref, *, alpha):
    mixed = _dot(sb_ref[...], w1_ref[...]) + _dot(gl_ref[...], w2_ref[...])
    x = _layer_norm(alpha * h_ref[...] + mixed, g1_ref[...], b1_ref[...])
    o_ref[...] = _swiglu_ln(x, win_ref, wout_ref, g2_ref[...], b2_ref[...], acc_ref, alpha)


def _prep_ffn(w_in, w_out):
    assert w_in.shape[1] == 2 * w_out.shape[0] and w_out.shape[0] % FFN_CHUNK == 0
    return w_in.astype(BF16), w_out.astype(BF16)


def _ffn_ln(x, ffn_w, g, b, alpha, tm=ROW_TILE):
    win, wout = ffn_w
    m, d = x.shape
    tm = min(tm, m)
    assert m % tm == 0
    row = pl.BlockSpec((tm, d), lambda i: (i, 0))
    return pl.pallas_call(
        functools.partial(_ffn_ln_kernel, alpha=alpha),
        grid=(m // tm,),
        in_specs=[row, _resident(win.shape), _resident(wout.shape),
                  _resident((1, d)), _resident((1, d))],
        out_specs=row,
        out_shape=jax.ShapeDtypeStruct((m, d), F32),
        scratch_shapes=[pltpu.VMEM((tm, d), F32)],
        compiler_params=_params(1),
        name="ffn_ln",
    )(x, win, wout, g.reshape(1, d), b.reshape(1, d))


def _outproj_ffn_ln(h, sb_o, gla_y, w1, w2, g1, b1, ffn_w, g2, b2, alpha, tm=ROW_TILE):
    win, wout = ffn_w
    m, d = h.shape
    tm = min(tm, m)
    assert m % tm == 0
    spec = lambda n: pl.BlockSpec((tm, n), lambda i: (i, 0))
    vec = lambda a: a.reshape(1, d)
    return pl.pallas_call(
        functools.partial(_outproj_ffn_ln_kernel, alpha=alpha),
        grid=(m // tm,),
        in_specs=[spec(d), spec(sb_o.shape[1]), spec(gla_y.shape[1]), _resident(w1.shape),
                  _resident(w2.shape), _resident((1, d)), _resident((1, d)),
                  _resident(win.shape), _resident(wout.shape),
                  _resident((1, d)), _resident((1, d))],
        out_specs=spec(d),
        out_shape=jax.ShapeDtypeStruct((m, d), F32),
        scratch_shapes=[pltpu.VMEM((tm, d), F32)],
        compiler_params=_params(1),
        name="outproj_ffn_ln",
    )(h, sb_o, gla_y, w1, w2, vec(g1), vec(b1), win, wout, vec(g2), vec(b2))


def _mix_kernel(x_ref, wsb_ref, wgl_ref, wlr_ref, wgu_ref, bg_ref,
                q_ref, kf_ref, vf_ref, kb_ref, vb_ref, gq_ref, gk_ref, gv_ref, gr_ref, lg_ref,
                *, sb_scale, gla_scale, feature_major):
    xb = x_ref[...].astype(BF16)
    w = q_ref.shape[1]
    kw = gq_ref.shape[1]
    gw = gv_ref.shape[1]
    q_ref[...] = (_dot(xb, wsb_ref[:, 0:w]) * sb_scale).astype(BF16)
    k = _dot(xb, wsb_ref[:, w:2 * w])
    kb_ref[...] = k.astype(BF16)
    v = _dot(xb, wsb_ref[:, 2 * w:3 * w])
    vb_ref[...] = v.astype(BF16)
    if feature_major:
        kf_ref[0] = k.T
        vf_ref[0] = v.T
    else:
        kf_ref[...] = k
        vf_ref[...] = v
    gq_ref[...] = _dot(xb, wgl_ref[:, 0:kw]) * gla_scale
    gk_ref[...] = _dot(xb, wgl_ref[:, kw:2 * kw])
    gv_ref[...] = _dot(xb, wgl_ref[:, 2 * kw:2 * kw + gw]).astype(BF16)
    gr_ref[...] = _dot(xb, wgl_ref[:, 2 * kw + gw:2 * kw + 2 * gw])
    low_rank = _dot(xb, wlr_ref[...]).astype(BF16)
    gate_logit = _dot(low_rank, wgu_ref[...]) + bg_ref[...]
    lg_ref[...] = _log_sigmoid(gate_logit) * (1.0 / GLA_TAU)


def _prep_mix(w_mix, w_gate_up, b_gate, w, kw, gw):
    rank = w_gate_up.shape[0]
    assert w_mix.shape[1] == 3 * w + 2 * kw + 2 * gw + rank and rank <= LANES
    wsb = w_mix[:, :3 * w].astype(BF16)
    wgl = w_mix[:, 3 * w:3 * w + 2 * kw + 2 * gw].astype(BF16)
    wlr = jnp.pad(w_mix[:, 3 * w + 2 * kw + 2 * gw:], ((0, 0), (0, LANES - rank))).astype(BF16)
    wgu = jnp.pad(w_gate_up, ((0, LANES - rank), (0, 0))).astype(BF16)
    return wsb, wgl, wlr, wgu, b_gate.reshape(1, kw).astype(F32)


def _mix(x, mix_w, w, kw, gw, sb_scale, gla_scale, tm=ROW_TILE, seqs=None):
    wsb, wgl, wlr, wgu, bg = mix_w
    m, d = x.shape
    tm = min(tm, m)
    assert m % tm == 0
    spec = lambda n: pl.BlockSpec((tm, n), lambda i: (i, 0))
    sds = lambda n, dt: jax.ShapeDtypeStruct((m, n), dt)
    kv_spec, kv_sds = spec(w), sds(w, F32)
    if seqs is not None:
        per_seq = m // (seqs * tm)
        assert per_seq * seqs * tm == m
        kv_spec = pl.BlockSpec((1, w, tm), lambda i: (i // per_seq, 0, i % per_seq))
        kv_sds = jax.ShapeDtypeStruct((seqs, w, m // seqs), F32)
    return pl.pallas_call(
        functools.partial(_mix_kernel, sb_scale=sb_scale, gla_scale=gla_scale,
                          feature_major=seqs is not None),
        grid=(m // tm,),
        in_specs=[spec(d), _resident(wsb.shape), _resident(wgl.shape), _resident(wlr.shape),
                  _resident(wgu.shape), _resident(bg.shape)],
        out_specs=[spec(w), kv_spec, kv_spec, spec(w), spec(w),
                   spec(kw), spec(kw), spec(gw), spec(gw), spec(kw)],
        out_shape=[sds(w, BF16), kv_sds, kv_sds, sds(w, BF16), sds(w, BF16),
                   sds(kw, F32), sds(kw, F32), sds(gw, BF16), sds(gw, F32), sds(kw, F32)],
        compiler_params=_params(1),
        name="mix_proj",
    )(x, wsb, wgl, wlr, wgu, bg)


def _sb_logs(z2, u, mask):
    neg_abs = lax.bitcast_convert_type(
        lax.bitcast_convert_type(z2, jnp.uint32) | jnp.uint32(0x80000000), F32)
    l2 = jnp.log(1.0 + jnp.exp2(neg_abs)) * LOG2E
    lb = jnp.minimum(z2, 0.0) - l2
    lk = lb - z2
    if mask is not None:
        lk = jnp.where(mask, lk, 0.0)
    n = z2.shape[1]
    s = _dot(lk.astype(BF16), u[:n, :n])
    return lb, lk[:, 0:1], s


def _sb_finish(logs, mask, c):
    lb, lk0, s = logs
    w = jnp.exp2(lb + s + c)
    if mask is not None:
        w = jnp.where(mask, w, 0.0)
    return w.astype(BF16), c + s[:, 0:1] + lk0


def _sb_prompt_kernel(bias_ref, q_ref, k_ref, v_ref, km_ref, vm_ref, u_ref, o_ref,
                      acc_ref, z_ref, w_ref, *, tq, n_meta, dh):
    kb = KEY_BLOCK
    hp = pl.program_id(1)
    i = pl.program_id(2)
    q = q_ref[0]
    lane = lax.broadcasted_iota(jnp.int32, q.shape, 1)
    zero = jnp.zeros_like(q)
    q0 = jnp.where(lane < dh, q, zero)
    q1 = jnp.where(lane >= dh, q, zero)
    qs = jnp.concatenate([q0[:kb], q1[:kb], q0[kb:], q1[kb:]], axis=0)
    row = lax.broadcasted_iota(jnp.int32, (2 * tq, 1), 0)
    second_head = ((row >= kb) & (row < 2 * kb)) | (row >= 3 * kb)
    token = jnp.where(row < kb, row, jnp.where(row < 3 * kb, row - kb, row - 2 * kb))
    bias = jnp.where(second_head, bias_ref[2 * hp + 1], bias_ref[2 * hp]) * LOG2E
    bias_hi = lax.bitcast_convert_type(
        lax.bitcast_convert_type(bias, jnp.uint32) & jnp.uint32(0xFFFF0000), F32)
    lane2 = lax.broadcasted_iota(jnp.int32, (2 * tq, LANES), 1)
    bias_lanes = (jnp.where(lane2 == 0, 1.0, 0.0) * bias_hi
                  + jnp.where(lane2 == 1, 1.0, 0.0) * (bias - bias_hi))
    qs = jnp.concatenate([qs, bias_lanes.astype(BF16)], axis=1)
    key_ones = jnp.ones((kb, LANES), BF16)
    qpos = i * tq + token
    u = u_ref[...]
    last_pair = i

    def rows(ref, pair, half):
        off = pl.multiple_of(pair * (2 * kb) + half * kb, kb)
        return ref[0, pl.ds(off, kb), :]

    def logits(pair, half, queries=qs):
        return _dot_nt(queries, jnp.concatenate([rows(k_ref, pair, half), key_ones], axis=1))

    def weighted_values(pair):
        return _dot(w_ref[1], rows(v_ref, pair, 1)) + _dot(w_ref[0], rows(v_ref, pair, 0))

    nm = km_ref.shape[0]
    meta_valid = lax.broadcasted_iota(jnp.int32, (2 * tq, nm), 1) < n_meta
    logs_meta = _sb_logs(_dot_nt(qs, jnp.concatenate([km_ref[...], key_ones[:nm]], axis=1)), u,
                         meta_valid)

    col = lax.broadcasted_iota(jnp.int32, (2 * tq, kb), 1)
    visible_lo = last_pair * (2 * kb) + col < qpos
    visible_hi = (last_pair * (2 * kb) + kb + col < qpos)[tq:]
    first_next = jnp.maximum(last_pair - 1, 0)
    logs_hi = _sb_logs(logits(last_pair, 1, qs[tq:]), u, visible_hi)
    z_ref[1] = logits(first_next, 1)
    logs_lo = _sb_logs(logits(last_pair, 0), u, visible_lo)
    z_ref[0] = logits(first_next, 0)
    acc_ref[...] = jnp.zeros_like(acc_ref)
    w_hi, c_hi = _sb_finish(logs_hi, visible_hi, jnp.zeros((tq, 1), F32))
    w_ref[1, :tq] = jnp.zeros((tq, kb), BF16)
    w_ref[1, tq:] = w_hi
    w_ref[0], c = _sb_finish(logs_lo, visible_lo,
                             jnp.concatenate([jnp.zeros((tq, 1), F32), c_hi], axis=0))

    def body(n, c):
        nxt = jnp.maximum(last_pair - 2 - n, 0)
        logs_hi = _sb_logs(z_ref[1], u, None)
        pv = weighted_values(last_pair - n)
        z_ref[1] = logits(nxt, 1)
        logs_lo = _sb_logs(z_ref[0], u, None)
        z_ref[0] = logits(nxt, 0)
        acc_ref[...] += pv
        w_ref[1], c = _sb_finish(logs_hi, None, c)
        w_ref[0], c = _sb_finish(logs_lo, None, c)
        return c

    c = lax.fori_loop(0, last_pair // 2, lambda m, c: body(2 * m + 1, body(2 * m, c)), c)
    c = lax.cond(last_pair % 2 == 1, lambda c: body(last_pair - 1, c), lambda c: c, c)
    pv = weighted_values(0)
    wm, _ = _sb_finish(logs_meta, meta_valid, c)
    acc = acc_ref[...] + pv + _dot(wm, vm_ref[...])
    first_head = lax.broadcasted_iota(jnp.int32, (kb, LANES), 1) < dh
    o_ref[0, :kb] = jnp.where(first_head, acc[:kb], acc[kb:2 * kb]).astype(BF16)
    o_ref[0, kb:] = jnp.where(first_head, acc[2 * kb:3 * kb], acc[3 * kb:]).astype(BF16)


def _suffix_matrix(n):
    r = lax.broadcasted_iota(jnp.int32, (n, n), 0)
    c = lax.broadcasted_iota(jnp.int32, (n, n), 1)
    return (r > c).astype(BF16)


def _sb_prompt(bias, q, k, v, km, vm, dh, tq=2 * KEY_BLOCK):
    b, t, w = q.shape
    assert t % tq == 0 and tq == 2 * KEY_BLOCK
    assert w % LANES == 0 and LANES == 2 * dh
    n_meta = km.shape[0]
    pad = ((0, LANES - n_meta), (0, 0))
    km = jnp.pad(km, pad)
    vm = jnp.pad(vm, pad)
    u = _suffix_matrix(KEY_BLOCK)
    qspec = pl.BlockSpec((1, tq, LANES), lambda bi, hp, i: (bi, i, hp))
    kvspec = pl.BlockSpec((1, t, LANES), lambda bi, hp, i: (bi, 0, hp))
    mspec = pl.BlockSpec((LANES, LANES), lambda bi, hp, i: (0, hp))
    return pl.pallas_call(
        functools.partial(_sb_prompt_kernel, tq=tq, n_meta=n_meta, dh=dh),
        grid=(b, w // LANES, t // tq),
        in_specs=[pl.BlockSpec(memory_space=pltpu.SMEM), qspec, kvspec, kvspec, mspec, mspec,
                  _resident(u.shape)],
        out_specs=qspec,
        out_shape=jax.ShapeDtypeStruct((b, t, w), BF16),
        scratch_shapes=[pltpu.VMEM((2 * tq, LANES), F32),
                        pltpu.VMEM((2, 2 * tq, KEY_BLOCK), F32),
                        pltpu.VMEM((2, 2 * tq, KEY_BLOCK), BF16)],
        compiler_params=_params(3),
        name="sb_prompt",
    )(bias, q, k, v, km, vm, u)


def _sb_sample_kernel(pt_ref, bias_ref, q_ref, kn_ref, vn_ref, *rest, n_group, heads):
    kp = rest[:n_group]
    vp = rest[n_group:2 * n_group]
    u_ref, o_ref, qbd_ref, c_ref, acc_ref = rest[2 * n_group:]
    p = pl.program_id(1)
    t_new, w = q_ref.shape[1], q_ref.shape[2]
    dh = w // heads
    ht = heads * t_new
    page = kp[0].shape[2]
    head_of_row = lax.broadcasted_iota(jnp.int32, (ht, w), 0) // t_new
    head_of_lane = lax.broadcasted_iota(jnp.int32, (ht, w), 1) // dh
    own = head_of_row == head_of_lane
    u = u_ref[...]
    bias = bias_ref[...] * LOG2E

    @pl.when(p == 0)
    def _start():
        qt = jnp.concatenate([q_ref[0]] * heads, axis=0)
        qbd = jnp.where(own, qt, 0.0).astype(BF16)
        qbd_ref[...] = qbd
        fill = jnp.zeros((LANES - t_new, w), F32)
        kn = jnp.concatenate([kn_ref[0], fill], axis=0).astype(BF16)
        vn = jnp.concatenate([vn_ref[0], fill], axis=0).astype(BF16)
        col = lax.broadcasted_iota(jnp.int32, (ht, LANES), 1)
        t_of_row = lax.broadcasted_iota(jnp.int32, (ht, LANES), 0) % t_new
        visible = col < t_of_row
        wn, c_ref[...] = _sb_finish(_sb_logs(_dot_nt(qbd, kn) + bias, u, visible), visible,
                                    jnp.zeros((ht, 1), F32))
        acc_ref[...] = _dot(wn, vn)

    qbd = qbd_ref[...]
    per_block = KEY_BLOCK // page
    blocks = [range(blk * per_block, (blk + 1) * per_block)
              for blk in reversed(range(n_group // per_block))]
    logs = [_sb_logs(jnp.concatenate([_dot(qbd, kp[g][0].astype(BF16)) for g in ids], axis=1)
                     + bias, u, None) for ids in blocks]
    c = c_ref[...]
    pv = None
    for ids, blk_logs in zip(blocks, logs):
        wb, c = _sb_finish(blk_logs, None, c)
        for n, g in enumerate(ids):
            part = _dot_nt(wb[:, n * page:(n + 1) * page], vp[g][0].astype(BF16))
            pv = part if pv is None else pv + part
    acc_ref[...] += pv
    c_ref[...] = c

    @pl.when(p == pl.num_programs(1) - 1)
    def _finish():
        acc = jnp.where(own, acc_ref[...], 0.0)
        out = acc[0:t_new]
        for h in range(1, heads):
            out = out + acc[h * t_new:(h + 1) * t_new]
        o_ref[0] = out


def _sb_sample(bias, q, k_new, v_new, cache_k, cache_v, page_table, heads):
    db, t_new, w = q.shape
    n_phys, _, page = cache_k.shape
    n_pages = page_table.shape[1]
    assert KEY_BLOCK % page == 0 and t_new <= LANES
    per_block = KEY_BLOCK // page
    n_group = PAGES_PER_STEP if n_pages % PAGES_PER_STEP == 0 else per_block
    assert n_pages % n_group == 0 and n_group % per_block == 0
    n_steps = n_pages // n_group
    ht = heads * t_new
    bias_rows = jnp.repeat(bias.astype(F32), t_new).reshape(ht, 1)
    u = _suffix_matrix(KEY_BLOCK)
    tok = pl.BlockSpec((1, t_new, w), lambda b, p, pt: (b, 0, 0))

    def page_spec(g):
        return pl.BlockSpec((1, w, page),
                            lambda b, p, pt: (pt[b, (n_steps - 1 - p) * n_group + g], 0, 0))

    const = lambda shape: pl.BlockSpec(shape, lambda b, p, pt: (0,) * len(shape))
    grid_spec = pltpu.PrefetchScalarGridSpec(
        num_scalar_prefetch=1,
        grid=(db, n_steps),
        in_specs=[const((ht, 1)), tok, tok, tok]
                 + [page_spec(g) for g in range(n_group)] * 2 + [const(u.shape)],
        out_specs=tok,
        scratch_shapes=[pltpu.VMEM((ht, w), BF16), pltpu.VMEM((ht, 1), F32),
                        pltpu.VMEM((ht, w), F32)],
    )
    return pl.pallas_call(
        functools.partial(_sb_sample_kernel, n_group=n_group, heads=heads),
        grid_spec=grid_spec,
        out_shape=jax.ShapeDtypeStruct((db, t_new, w), F32),
        compiler_params=_params(2),
        name="sb_sample",
    )(page_table, bias_rows, q, k_new, v_new, *([cache_k] * n_group), *([cache_v] * n_group), u)


def _gla_kernel(q_ref, k_ref, lg_ref, v_ref, r_ref, s0_ref, gn_ref, ltri_ref, ones_ref, y_ref,
                sout_ref, st_ref, *, chunk, sub, heads):
    t = pl.program_id(1)
    nb, tb, kw = q_ref.shape
    gw = v_ref.shape[2]
    dk, dv = kw // heads, gw // heads
    n_sub = chunk // sub
    n_chunks = tb // chunk
    group = GLA_GROUP if n_chunks % GLA_GROUP == 0 else 1

    @pl.when(t == 0)
    def _init():
        for bi in range(nb):
            for h in range(heads):
                blocks = [s0_ref[bi, h] if g == h else jnp.zeros((dk, dv), F32)
                          for g in range(heads)]
                st_ref[bi, h * dk:(h + 1) * dk, :] = jnp.concatenate(blocks, axis=1)

    own_state = (lax.broadcasted_iota(jnp.int32, (kw, gw), 0) // dk
                 == lax.broadcasted_iota(jnp.int32, (kw, gw), 1) // dv)
    ones = ones_ref[...]
    lane_head = lax.broadcasted_iota(jnp.int32, (sub, kw), 1) // dk
    srow = lax.broadcasted_iota(jnp.int32, (chunk, kw), 0)
    att_col = lax.broadcasted_iota(jnp.int32, (heads * sub, chunk), 1)
    att_t = lax.broadcasted_iota(jnp.int32, (heads * sub, chunk), 0) % sub
    ltri = ltri_ref[...]
    gn = gn_ref[...]

    def state_free(rows_list):
        n = range(len(rows_list))
        q = [q_ref[bi, r, :] for bi, r in rows_list]
        k = [k_ref[bi, r, :] for bi, r in rows_list]
        v = [v_ref[bi, r, :] for bi, r in rows_list]
        lg = [lg_ref[bi, r, :] for bi, r in rows_list]
        lg_hi = [x.astype(BF16) for x in lg]
        lg_lo = [(x - h.astype(F32)).astype(BF16) for x, h in zip(lg, lg_hi)]
        bc = [_dot(ltri, h) + _dot(ltri, l) for h, l in zip(lg_hi, lg_lo)]
        b_last = [x[chunk - 1:chunk, :] for x in bc]
        k_last = [(k[j] * jnp.exp(b_last[j] - bc[j])).astype(BF16) for j in n]
        update = [jnp.where(own_state, _dot_tn(k_last[j], v[j]), 0.0) for j in n]
        total = [_dot_tn(lg_hi[j], ones) + _dot_tn(lg_lo[j], ones) for j in n]
        decay = [jnp.concatenate([jnp.exp(x)] * (gw // LANES), axis=1) for x in total]
        q_decayed = [(q[j] * jnp.exp(bc[j])).astype(BF16) for j in n]
        intra = [[] for _ in n]
        for sc in range(n_sub):
            lo, hi = sc * sub, (sc + 1) * sub
            seen = srow < hi
            att = []
            for j in n:
                ref = bc[j][lo - 1:lo, :] if sc > 0 else jnp.zeros((1, kw), F32)
                qd = q[j][lo:hi] * jnp.exp(bc[j][lo:hi] - ref)
                kd = (jnp.where(seen, k[j], 0.0)
                      * jnp.exp(jnp.where(seen, ref - bc[j], 0.0))).astype(BF16)
                qs = jnp.concatenate([jnp.where(lane_head == h, qd, 0.0) for h in range(heads)],
                                     axis=0).astype(BF16)
                att.append(_dot_nt(qs, kd))
            for j in n:
                a = jnp.where(att_col <= lo + att_t, att[j], 0.0).astype(BF16)
                ov = _dot(a, v[j])
                intra[j].append(jnp.concatenate(
                    [ov[h * sub:(h + 1) * sub, h * dv:(h + 1) * dv] for h in range(heads)],
                    axis=1))
        return [(q_decayed[j], jnp.concatenate(intra[j], axis=0), decay[j], update[j]) for j in n]

    def finish(rows_list, outs):
        heads_of = [[o[:, h * dv:(h + 1) * dv] for h in range(heads)] for o in outs]
        scale = [[lax.rsqrt(jnp.mean(oh * oh, axis=-1, keepdims=True) + RMS_EPS) for oh in ohs]
                 for ohs in heads_of]
        for (bi, rows), ohs, scs in zip(rows_list, heads_of, scale):
            r = r_ref[bi, rows, :]
            normed = jnp.concatenate([oh * sc for oh, sc in zip(ohs, scs)], axis=1)
            y_ref[bi, rows, :] = (normed * gn * (r * jax.nn.sigmoid(r))).astype(BF16)

    def group_body(gi, carry):
        rows = [(bi, pl.ds(pl.multiple_of((gi * group + j) * chunk, chunk), chunk))
                for j in range(group) for bi in range(nb)]
        parts = state_free(rows)
        st = [st_ref[bi] for bi in range(nb)]
        outs = []
        for (bi, _), (q_decayed, intra, decay, update) in zip(rows, parts):
            outs.append(_dot(q_decayed, st[bi].astype(BF16)) + intra)
            st[bi] = st[bi] * decay + update
        for bi in range(nb):
            st_ref[bi] = st[bi]
        finish(rows, outs)
        return carry

    lax.fori_loop(0, tb // (chunk * group), group_body, 0)

    @pl.when(t == pl.num_programs(1) - 1)
    def _final():
        for bi in range(nb):
            for h in range(heads):
                sout_ref[bi, h] = st_ref[bi, h * dk:(h + 1) * dk, h * dv:(h + 1) * dv]


def _gla(q, k, lg, v, r, s0, g_norm, heads, chunk, tb, nb=1):
    b, t, kw = q.shape
    gw = v.shape[2]
    dk, dv = kw // heads, gw // heads
    sub = min(GLA_SUB, chunk)
    tb = min(tb, t)
    assert t % tb == 0 and tb % chunk == 0 and chunk % sub == 0 and b % nb == 0
    shared_state = s0.shape[0] == 1
    assert not (shared_state and nb > 1)
    ltri = (lax.broadcasted_iota(jnp.int32, (chunk, chunk), 0)
            >= lax.broadcasted_iota(jnp.int32, (chunk, chunk), 1)).astype(BF16)
    gn = jnp.tile(g_norm.astype(F32), heads).reshape(1, gw)
    tok = lambda n: pl.BlockSpec((nb, tb, n), lambda bi, ti: (bi, ti, 0))
    state_in = pl.BlockSpec((nb, heads, dk, dv),
                            lambda bi, ti: (0 if shared_state else bi, 0, 0, 0))
    state_out = pl.BlockSpec((nb, heads, dk, dv), lambda bi, ti: (bi, 0, 0, 0))
    ones = jnp.ones((chunk, LANES), BF16)
    const = lambda shape: pl.BlockSpec(shape, lambda bi, ti: (0,) * len(shape))
    return pl.pallas_call(
        functools.partial(_gla_kernel, chunk=chunk, sub=sub, heads=heads),
        grid=(b // nb, t // tb),
        in_specs=[tok(kw), tok(kw), tok(kw), tok(gw), tok(gw), state_in, const((1, gw)),
                  const((chunk, chunk)), const((chunk, LANES))],
        out_specs=[tok(gw), state_out],
        out_shape=[jax.ShapeDtypeStruct((b, t, gw), BF16),
                   jax.ShapeDtypeStruct((b, heads, dk, dv), F32)],
        scratch_shapes=[pltpu.VMEM((nb, kw, gw), F32)],
        compiler_params=_params(2),
        name="gla",
    )(q, k, lg, v, r, s0, gn, ltri, ones)


def kernel(x_prompt, x_sample, cache_k, cache_v, state_gla, page_table, meta_tokens, ln_g, ln_b,
           w_ffn1_in, w_ffn1_out, w_mix_in, w_gate_up, b_gate, b_sb, g_gla_norm, w_out,
           w_ffn2_in, w_ffn2_out):
    depth = w_out.shape[0]
    assert depth == 1
    alpha = (2.0 * depth) ** 0.25
    b, t, d = x_prompt.shape
    db, ts, _ = x_sample.shape
    _, n_phys, page, heads, dh = cache_k.shape
    w = heads * dh
    _, _, gheads, dk, dv = state_gla.shape
    kw, gw = gheads * dk, gheads * dv
    n_meta = meta_tokens.shape[0]
    lyr = 0

    ffn1 = _prep_ffn(w_ffn1_in[lyr], w_ffn1_out[lyr])
    ffn2 = _prep_ffn(w_ffn2_in[lyr], w_ffn2_out[lyr])
    mix_w = _prep_mix(w_mix_in[lyr], w_gate_up[lyr], b_gate[lyr], w, kw, gw)
    wo_sb = w_out[lyr][:w].astype(BF16)
    wo_gla = w_out[lyr][w:].astype(BF16)
    g, bb = ln_g[lyr], ln_b[lyr]
    bias = b_sb[lyr].astype(F32)
    mix = functools.partial(_mix, mix_w=mix_w, w=w, kw=kw, gw=gw, sb_scale=dh ** -0.5 * LOG2E,
                            gla_scale=dk ** -0.5)

    hm = _ffn_ln(meta_tokens.astype(x_prompt.dtype), ffn1, g[0], bb[0], alpha)
    hp = _ffn_ln(x_prompt.reshape(b * t, d), ffn1, g[0], bb[0], alpha)
    hs = _ffn_ln(x_sample.reshape(db * ts, d), ffn1, g[0], bb[0], alpha)
    _, kmf, vmf, kmb, vmb, gqm, gkm, gvm, grm, lgm = mix(hm)
    qp, kpt, vpt, kpb, vpb, gqp, gkp, gvp, grp, lgp = mix(hp, seqs=b)
    qs, ksf, vsf, _, _, gqs, gks, gvs, grs, lgs = mix(hs)

    r3 = lambda a, n: a.reshape(n, a.shape[0] // n, a.shape[1])
    sb_p = _sb_prompt(bias, r3(qp, b), r3(kpb, b), r3(vpb, b), kmb, vmb, dh)
    zero_state = jnp.zeros((1, gheads, dk, dv), F32)
    _, st_meta = _gla(r3(gqm, 1), r3(gkm, 1), r3(lgm, 1), r3(gvm, 1), r3(grm, 1), zero_state,
                      g_gla_norm[lyr], gheads, n_meta, n_meta)
    gla_p, st_p = _gla(r3(gqp, b), r3(gkp, b), r3(lgp, b), r3(gvp, b), r3(grp, b), st_meta,
                       g_gla_norm[lyr], gheads, GLA_CHUNK, GLA_GROUP * GLA_CHUNK)
    y_prompt = _outproj_ffn_ln(hp, sb_p.reshape(b * t, w), gla_p.reshape(b * t, gw), wo_sb,
                               wo_gla, g[1], bb[1], ffn2, g[2], bb[2], alpha).reshape(b, t, d)

    feature_major = lambda c: c.transpose(0, 2, 3, 1).reshape(n_phys, w, page)
    sb_s = _sb_sample(bias, r3(qs, db).astype(F32), r3(ksf, db), r3(vsf, db),
                      feature_major(cache_k[lyr]), feature_major(cache_v[lyr]), page_table, heads)
    ts_pad = -(-ts // GLA_SUB) * GLA_SUB
    padt = lambda a: jnp.pad(r3(a, db), ((0, 0), (0, ts_pad - ts), (0, 0)))
    gla_s, st_s = _gla(padt(gqs), padt(gks), padt(lgs), padt(gvs), padt(grs),
                       state_gla[lyr].astype(F32), g_gla_norm[lyr], gheads, ts_pad, ts_pad,
                       nb=GLA_SEQS_PER_STEP if db % GLA_SEQS_PER_STEP == 0 else 1)
    y_sample = _outproj_ffn_ln(hs, sb_s.reshape(db * ts, w).astype(BF16),
                               gla_s[:, :ts].reshape(db * ts, gw), wo_sb, wo_gla,
                               g[1], bb[1], ffn2, g[2], bb[2], alpha).reshape(db, ts, d)

    def with_meta(meta_rows, feature_major_rows):
        full = jnp.concatenate(
            [jnp.broadcast_to(meta_rows.T[None], (b, w, n_meta)), feature_major_rows], axis=2)
        return full.reshape(1, b, heads, dh, n_meta + t).transpose(0, 1, 4, 2, 3)

    sdt = state_gla.dtype
    return (y_prompt, y_sample, with_meta(kmf, kpt), with_meta(vmf, vpt),
            st_p.astype(sdt)[None],
            ksf.reshape(1, db, ts, heads, dh), vsf.reshape(1, db, ts, heads, dh),
            st_s.astype(sdt)[None])
```

```python
import functools

import jax
import jax.numpy as jnp
from jax import lax
from jax.experimental import pallas as pl
from jax.experimental.pallas import tpu as pltpu

F32 = jnp.float32
BF16 = jnp.bfloat16

LN_EPS = 1e-5
RMS_EPS = 1e-6
GLA_TAU = 16.0
GLA_CHUNK = 64
GLA_SUB = 16
LANES = 128
MXU_TILE = 256
KEY_BLOCK = MXU_TILE
FFN_CHUNK = MXU_TILE
ROW_TILE = 512
PAGES_PER_STEP = 32
GLA_GROUP = 16
GLA_SEQS_PER_STEP = 8
LOG2E = 1.4426950408889634
VMEM_LIMIT_BYTES = 56 * 1024 * 1024


def _dot(a, b):
    return jnp.dot(a, b, preferred_element_type=F32)


def _dot_nt(a, b):
    return lax.dot_general(a, b, (((1,), (1,)), ((), ())), preferred_element_type=F32)


def _dot_tn(a, b):
    return lax.dot_general(a, b, (((0,), (0,)), ((), ())), preferred_element_type=F32)


def _params(n_grid):
    return pltpu.CompilerParams(dimension_semantics=("arbitrary",) * n_grid,
                                vmem_limit_bytes=VMEM_LIMIT_BYTES)


def _resident(shape):
    return pl.BlockSpec(shape, lambda *_: (0,) * len(shape), pipeline_mode=pl.Buffered(1))


def _layer_norm(y, g, b):
    mu = jnp.mean(y, axis=-1, keepdims=True)
    d = y - mu
    var = jnp.mean(d * d, axis=-1, keepdims=True)
    return d * lax.rsqrt(var + LN_EPS) * g + b


def _log_sigmoid(z):
    return jnp.minimum(z, 0.0) - jnp.log(1.0 + jnp.exp(-jnp.abs(z)))


def _swiglu_ln(x, win_ref, wout_ref, g, b, acc_ref, alpha):
    xb = x.astype(BF16)
    f = wout_ref.shape[0]
    for c in range(f // FFN_CHUNK):
        lo, hi = c * FFN_CHUNK, (c + 1) * FFN_CHUNK
        gate = _dot(xb, win_ref[:, lo:hi])
        up = _dot(xb, win_ref[:, f + lo:f + hi])
        act = (gate * jax.nn.sigmoid(gate) * up).astype(BF16)
        part = _dot(act, wout_ref[lo:hi, :])
        if c == 0:
            acc_ref[...] = part
        else:
            acc_ref[...] += part
    return _layer_norm(alpha * x + 0.5 * acc_ref[...], g, b)


def _ffn_ln_kernel(x_ref, win_ref, wout_ref, g_ref, b_ref, o_ref, acc_ref, *, alpha):
    o_ref[...] = _swiglu_ln(x_ref[...], win_ref, wout_ref, g_ref[...], b_ref[...], acc_ref, alpha)


def _outproj_ffn_ln_kernel(h_ref, sb_ref, gl_ref, w1_ref, w2_ref, g1_ref, b1_ref,
                           win_ref, wout_ref, g2_ref, b2_ref, o_ref, acc_ref, *, alpha):
    mixed = _dot(sb_ref[...], w1_ref[...]) + _dot(gl_ref[...], w2_ref[...])
    x = _layer_norm(alpha * h_ref[...] + mixed, g1_ref[...], b1_ref[...])
    o_ref[...] = _swiglu_ln(x, win_ref, wout_ref, g2_ref[...], b2_ref[...], acc_ref, alpha)


def _prep_ffn(w_in, w_out):
    assert w_in.shape[1] == 2 * w_out.shape[0] and w_out.shape[0] % FFN_CHUNK == 0
    return w_in.astype(BF16), w_out.astype(BF16)


def _ffn_ln(x, ffn_w, g, b, alpha, tm=ROW_TILE):
    win, wout = ffn_w
    m, d = x.shape
    tm = min(tm, m)
    assert m % tm == 0
    row = pl.BlockSpec((tm, d), lambda i: (i, 0))
    return pl.pallas_call(
        functools.partial(_ffn_ln_kernel, alpha=alpha),
        grid=(m // tm,),
        in_specs=[row, _resident(win.shape), _resident(wout.shape),
                  _resident((1, d)), _resident((1, d))],
        out_specs=row,
        out_shape=jax.ShapeDtypeStruct((m, d), F32),
        scratch_shapes=[pltpu.VMEM((tm, d), F32)],
        compiler_params=_params(1),
        name="ffn_ln",
    )(x, win, wout, g.reshape(1, d), b.reshape(1, d))


def _outproj_ffn_ln(h, sb_o, gla_y, w1, w2, g1, b1, ffn_w, g2, b2, alpha, tm=ROW_TILE):
    win, wout = ffn_w
    m, d = h.shape
    tm = min(tm, m)
    assert m % tm == 0
    spec = lambda n: pl.BlockSpec((tm, n), lambda i: (i, 0))
    vec = lambda a: a.reshape(1, d)
    return pl.pallas_call(
        functools.partial(_outproj_ffn_ln_kernel, alpha=alpha),
        grid=(m // tm,),
        in_specs=[spec(d), spec(sb_o.shape[1]), spec(gla_y.shape[1]), _resident(w1.shape),
                  _resident(w2.shape), _resident((1, d)), _resident((1, d)),
                  _resident(win.shape), _resident(wout.shape),
                  _resident((1, d)), _resident((1, d))],
        out_specs=spec(d),
        out_shape=jax.ShapeDtypeStruct((m, d), F32),
        scratch_shapes=[pltpu.VMEM((tm, d), F32)],
        compiler_params=_params(1),
        name="outproj_ffn_ln",
    )(h, sb_o, gla_y, w1, w2, vec(g1), vec(b1), win, wout, vec(g2), vec(b2))


def _mix_kernel(x_ref, wsb_ref, wgl_ref, wlr_ref, wgu_ref, bg_ref,
                q_ref, kf_ref, vf_ref, kb_ref, vb_ref, gq_ref, gk_ref, gv_ref, gr_ref, lg_ref,
                *, sb_scale, gla_scale, feature_major):
    xb = x_ref[...].astype(BF16)
    w = q_ref.shape[1]
    kw = gq_ref.shape[1]
    gw = gv_ref.shape[1]
    q_ref[...] = (_dot(xb, wsb_ref[:, 0:w]) * sb_scale).astype(BF16)
    k = _dot(xb, wsb_ref[:, w:2 * w])
    kb_ref[...] = k.astype(BF16)
    v = _dot(xb, wsb_ref[:, 2 * w:3 * w])
    vb_ref[...] = v.astype(BF16)
    if feature_major:
        kf_ref[0] = k.T
        vf_ref[0] = v.T
    else:
        kf_ref[...] = k
        vf_ref[...] = v
    gq_ref[...] = _dot(xb, wgl_ref[:, 0:kw]) * gla_scale
    gk_ref[...] = _dot(xb, wgl_ref[:, kw:2 * kw])
    gv_ref[...] = _dot(xb, wgl_ref[:, 2 * kw:2 * kw + gw]).astype(BF16)
    gr_ref[...] = _dot(xb, wgl_ref[:, 2 * kw + gw:2 * kw + 2 * gw])
    low_rank = _dot(xb, wlr_ref[...]).astype(BF16)
    gate_logit = _dot(low_rank, wgu_ref[...]) + bg_ref[...]
    lg_ref[...] = _log_sigmoid(gate_logit) * (1.0 / GLA_TAU)


def _prep_mix(w_mix, w_gate_up, b_gate, w, kw, gw):
    rank = w_gate_up.shape[0]
    assert w_mix.shape[1] == 3 * w + 2 * kw + 2 * gw + rank and rank <= LANES
    wsb = w_mix[:, :3 * w].astype(BF16)
    wgl = w_mix[:, 3 * w:3 * w + 2 * kw + 2 * gw].astype(BF16)
    wlr = jnp.pad(w_mix[:, 3 * w + 2 * kw + 2 * gw:], ((0, 0), (0, LANES - rank))).astype(BF16)
    wgu = jnp.pad(w_gate_up, ((0, LANES - rank), (0, 0))).astype(BF16)
    return wsb, wgl, wlr, wgu, b_gate.reshape(1, kw).astype(F32)


def _mix(x, mix_w, w, kw, gw, sb_scale, gla_scale, tm=2 * ROW_TILE, seqs=None):
    wsb, wgl, wlr, wgu, bg = mix_w
    m, d = x.shape
    tm = min(tm, m)
    assert m % tm == 0
    spec = lambda n: pl.BlockSpec((tm, n), lambda i: (i, 0))
    sds = lambda n, dt: jax.ShapeDtypeStruct((m, n), dt)
    kv_spec, kv_sds = spec(w), sds(w, F32)
    if seqs is not None:
        per_seq = m // (seqs * tm)
        assert per_seq * seqs * tm == m
        kv_spec = pl.BlockSpec((1, w, tm), lambda i: (i // per_seq, 0, i % per_seq))
        kv_sds = jax.ShapeDtypeStruct((seqs, w, m // seqs), F32)
    return pl.pallas_call(
        functools.partial(_mix_kernel, sb_scale=sb_scale, gla_scale=gla_scale,
                          feature_major=seqs is not None),
        grid=(m // tm,),
        in_specs=[spec(d), _resident(wsb.shape), _resident(wgl.shape), _resident(wlr.shape),
                  _resident(wgu.shape), _resident(bg.shape)],
        out_specs=[spec(w), kv_spec, kv_spec, spec(w), spec(w),
                   spec(kw), spec(kw), spec(gw), spec(gw), spec(kw)],
        out_shape=[sds(w, BF16), kv_sds, kv_sds, sds(w, BF16), sds(w, BF16),
                   sds(kw, F32), sds(kw, F32), sds(gw, BF16), sds(gw, F32), sds(kw, F32)],
        compiler_params=_params(1),
        name="mix_proj",
    )(x, wsb, wgl, wlr, wgu, bg)


def _sb_logs(z2, u, mask):
    neg_abs = lax.bitcast_convert_type(
        lax.bitcast_convert_type(z2, jnp.uint32) | jnp.uint32(0x80000000), F32)
    l2 = jnp.log(1.0 + jnp.exp2(neg_abs)) * LOG2E
    lb = jnp.minimum(z2, 0.0) - l2
    lk = lb - z2
    if mask is not None:
        lk = jnp.where(mask, lk, 0.0)
    n = z2.shape[1]
    s = _dot(lk.astype(BF16), u[:n, :n])
    return lb, lk[:, 0:1], s


def _sb_finish(logs, mask, c):
    lb, lk0, s = logs
    w = jnp.exp2(lb + s + c)
    if mask is not None:
        w = jnp.where(mask, w, 0.0)
    return w.astype(BF16), c + s[:, 0:1] + lk0


def _sb_prompt_kernel(bias_ref, q_ref, k_ref, v_ref, km_ref, vm_ref, u_ref, o_ref,
                      acc_ref, z_ref, w_ref, *, tq, n_meta, dh):
    kb = KEY_BLOCK
    hp = pl.program_id(1)
    i = pl.program_id(2)
    q = q_ref[0]
    lane = lax.broadcasted_iota(jnp.int32, q.shape, 1)
    zero = jnp.zeros_like(q)
    q0 = jnp.where(lane < dh, q, zero)
    q1 = jnp.where(lane >= dh, q, zero)
    qs = jnp.concatenate([q0[:kb], q1[:kb], q0[kb:], q1[kb:]], axis=0)
    row = lax.broadcasted_iota(jnp.int32, (2 * tq, 1), 0)
    second_head = ((row >= kb) & (row < 2 * kb)) | (row >= 3 * kb)
    token = jnp.where(row < kb, row, jnp.where(row < 3 * kb, row - kb, row - 2 * kb))
    bias = jnp.where(second_head, bias_ref[2 * hp + 1], bias_ref[2 * hp]) * LOG2E
    bias_hi = lax.bitcast_convert_type(
        lax.bitcast_convert_type(bias, jnp.uint32) & jnp.uint32(0xFFFF0000), F32)
    lane2 = lax.broadcasted_iota(jnp.int32, (2 * tq, LANES), 1)
    bias_lanes = (jnp.where(lane2 == 0, 1.0, 0.0) * bias_hi
                  + jnp.where(lane2 == 1, 1.0, 0.0) * (bias - bias_hi))
    qs = jnp.concatenate([qs, bias_lanes.astype(BF16)], axis=1)
    key_ones = jnp.ones((kb, LANES), BF16)
    qpos = i * tq + token
    u = u_ref[...]
    last_pair = i

    def rows(ref, pair, half):
        off = pl.multiple_of(pair * (2 * kb) + half * kb, kb)
        return ref[0, pl.ds(off, kb), :]

    def logits(pair, half, queries=qs):
        return _dot_nt(queries, jnp.concatenate([rows(k_ref, pair, half), key_ones], axis=1))

    def weighted_values(pair):
        return _dot(w_ref[1], rows(v_ref, pair, 1)) + _dot(w_ref[0], rows(v_ref, pair, 0))

    nm = km_ref.shape[0]
    meta_valid = lax.broadcasted_iota(jnp.int32, (2 * tq, nm), 1) < n_meta
    logs_meta = _sb_logs(_dot_nt(qs, jnp.concatenate([km_ref[...], key_ones[:nm]], axis=1)), u,
                         meta_valid)

    col = lax.broadcasted_iota(jnp.int32, (2 * tq, kb), 1)
    visible_lo = last_pair * (2 * kb) + col < qpos
    visible_hi = (last_pair * (2 * kb) + kb + col < qpos)[tq:]
    first_next = jnp.maximum(last_pair - 1, 0)
    logs_hi = _sb_logs(logits(last_pair, 1, qs[tq:]), u, visible_hi)
    z_ref[1] = logits(first_next, 1)
    logs_lo = _sb_logs(logits(last_pair, 0), u, visible_lo)
    z_ref[0] = logits(first_next, 0)
    acc_ref[...] = jnp.zeros_like(acc_ref)
    w_hi, c_hi = _sb_finish(logs_hi, visible_hi, jnp.zeros((tq, 1), F32))
    w_ref[1, :tq] = jnp.zeros((tq, kb), BF16)
    w_ref[1, tq:] = w_hi
    w_ref[0], c = _sb_finish(logs_lo, visible_lo,
                             jnp.concatenate([jnp.zeros((tq, 1), F32), c_hi], axis=0))

    def body(n, c):
        nxt = jnp.maximum(last_pair - 2 - n, 0)
        logs_hi = _sb_logs(z_ref[1], u, None)
        pv = weighted_values(last_pair - n)
        z_ref[1] = logits(nxt, 1)
        logs_lo = _sb_logs(z_ref[0], u, None)
        z_ref[0] = logits(nxt, 0)
        acc_ref[...] += pv
        w_ref[1], c = _sb_finish(logs_hi, None, c)
        w_ref[0], c = _sb_finish(logs_lo, None, c)
        return c

    c = lax.fori_loop(0, last_pair // 2, lambda m, c: body(2 * m + 1, body(2 * m, c)), c)
    c = lax.cond(last_pair % 2 == 1, lambda c: body(last_pair - 1, c), lambda c: c, c)
    pv = weighted_values(0)
    wm, _ = _sb_finish(logs_meta, meta_valid, c)
    acc = acc_ref[...] + pv + _dot(wm, vm_ref[...])
    first_head = lax.broadcasted_iota(jnp.int32, (kb, LANES), 1) < dh
    o_ref[0, :kb] = jnp.where(first_head, acc[:kb], acc[kb:2 * kb]).astype(BF16)
    o_ref[0, kb:] = jnp.where(first_head, acc[2 * kb:3 * kb], acc[3 * kb:]).astype(BF16)


def _suffix_matrix(n):
    r = lax.broadcasted_iota(jnp.int32, (n, n), 0)
    c = lax.broadcasted_iota(jnp.int32, (n, n), 1)
    return (r > c).astype(BF16)


def _sb_prompt(bias, q, k, v, km, vm, dh, tq=2 * KEY_BLOCK):
    b, t, w = q.shape
    assert t % tq == 0 and tq == 2 * KEY_BLOCK
    assert w % LANES == 0 and LANES == 2 * dh
    n_meta = km.shape[0]
    pad = ((0, LANES - n_meta), (0, 0))
    km = jnp.pad(km, pad)
    vm = jnp.pad(vm, pad)
    u = _suffix_matrix(KEY_BLOCK)
    qspec = pl.BlockSpec((1, tq, LANES), lambda bi, hp, i: (bi, i, hp))
    kvspec = pl.BlockSpec((1, t, LANES), lambda bi, hp, i: (bi, 0, hp))
    mspec = pl.BlockSpec((LANES, LANES), lambda bi, hp, i: (0, hp))
    return pl.pallas_call(
        functools.partial(_sb_prompt_kernel, tq=tq, n_meta=n_meta, dh=dh),
        grid=(b, w // LANES, t // tq),
        in_specs=[pl.BlockSpec(memory_space=pltpu.SMEM), qspec, kvspec, kvspec, mspec, mspec,
                  _resident(u.shape)],
        out_specs=qspec,
        out_shape=jax.ShapeDtypeStruct((b, t, w), BF16),
        scratch_shapes=[pltpu.VMEM((2 * tq, LANES), F32),
                        pltpu.VMEM((2, 2 * tq, KEY_BLOCK), F32),
                        pltpu.VMEM((2, 2 * tq, KEY_BLOCK), BF16)],
        compiler_params=_params(3),
        name="sb_prompt",
    )(bias, q, k, v, km, vm, u)


def _sb_sample_kernel(pt_ref, bias_ref, q_ref, kn_ref, vn_ref, *rest, n_group, heads):
    kp = rest[:n_group]
    vp = rest[n_group:2 * n_group]
    u_ref, o_ref, qbd_ref, c_ref, acc_ref = rest[2 * n_group:]
    p = pl.program_id(1)
    t_new, w = q_ref.shape[1], q_ref.shape[2]
    dh = w // heads
    ht = heads * t_new
    page = kp[0].shape[2]
    head_of_row = lax.broadcasted_iota(jnp.int32, (ht, w), 0) // t_new
    head_of_lane = lax.broadcasted_iota(jnp.int32, (ht, w), 1) // dh
    own = head_of_row == head_of_lane
    u = u_ref[...]
    bias = bias_ref[...] * LOG2E

    @pl.when(p == 0)
    def _start():
        qt = jnp.concatenate([q_ref[0]] * heads, axis=0)
        qbd = jnp.where(own, qt, 0.0).astype(BF16)
        qbd_ref[...] = qbd
        fill = jnp.zeros((LANES - t_new, w), F32)
        kn = jnp.concatenate([kn_ref[0], fill], axis=0).astype(BF16)
        vn = jnp.concatenate([vn_ref[0], fill], axis=0).astype(BF16)
        col = lax.broadcasted_iota(jnp.int32, (ht, LANES), 1)
        t_of_row = lax.broadcasted_iota(jnp.int32, (ht, LANES), 0) % t_new
        visible = col < t_of_row
        wn, c_ref[...] = _sb_finish(_sb_logs(_dot_nt(qbd, kn) + bias, u, visible), visible,
                                    jnp.zeros((ht, 1), F32))
        acc_ref[...] = _dot(wn, vn)

    qbd = qbd_ref[...]
    per_block = KEY_BLOCK // page
    blocks = [range(blk * per_block, (blk + 1) * per_block)
              for blk in reversed(range(n_group // per_block))]
    logs = [_sb_logs(jnp.concatenate([_dot(qbd, kp[g][0].astype(BF16)) for g in ids], axis=1)
                     + bias, u, None) for ids in blocks]
    c = c_ref[...]
    pv = None
    for ids, blk_logs in zip(blocks, logs):
        wb, c = _sb_finish(blk_logs, None, c)
        for n, g in enumerate(ids):
            part = _dot_nt(wb[:, n * page:(n + 1) * page], vp[g][0].astype(BF16))
            pv = part if pv is None else pv + part
    acc_ref[...] += pv
    c_ref[...] = c

    @pl.when(p == pl.num_programs(1) - 1)
    def _finish():
        acc = jnp.where(own, acc_ref[...], 0.0)
        out = acc[0:t_new]
        for h in range(1, heads):
            out = out + acc[h * t_new:(h + 1) * t_new]
        o_ref[0] = out


def _sb_sample(bias, q, k_new, v_new, cache_k, cache_v, page_table, heads):
    db, t_new, w = q.shape
    n_phys, _, page = cache_k.shape
    n_pages = page_table.shape[1]
    assert KEY_BLOCK % page == 0 and t_new <= LANES
    per_block = KEY_BLOCK // page
    n_group = PAGES_PER_STEP if n_pages % PAGES_PER_STEP == 0 else per_block
    assert n_pages % n_group == 0 and n_group % per_block == 0
    n_steps = n_pages // n_group
    ht = heads * t_new
    bias_rows = jnp.repeat(bias.astype(F32), t_new).reshape(ht, 1)
    u = _suffix_matrix(KEY_BLOCK)
    tok = pl.BlockSpec((1, t_new, w), lambda b, p, pt: (b, 0, 0))

    def page_spec(g):
        return pl.BlockSpec((1, w, page),
                            lambda b, p, pt: (pt[b, (n_steps - 1 - p) * n_group + g], 0, 0))

    const = lambda shape: pl.BlockSpec(shape, lambda b, p, pt: (0,) * len(shape))
    grid_spec = pltpu.PrefetchScalarGridSpec(
        num_scalar_prefetch=1,
        grid=(db, n_steps),
        in_specs=[const((ht, 1)), tok, tok, tok]
                 + [page_spec(g) for g in range(n_group)] * 2 + [const(u.shape)],
        out_specs=tok,
        scratch_shapes=[pltpu.VMEM((ht, w), BF16), pltpu.VMEM((ht, 1), F32),
                        pltpu.VMEM((ht, w), F32)],
    )
    return pl.pallas_call(
        functools.partial(_sb_sample_kernel, n_group=n_group, heads=heads),
        grid_spec=grid_spec,
        out_shape=jax.ShapeDtypeStruct((db, t_new, w), F32),
        compiler_params=_params(2),
        name="sb_sample",
    )(page_table, bias_rows, q, k_new, v_new, *([cache_k] * n_group), *([cache_v] * n_group), u)


def _gla_kernel(q_ref, k_ref, lg_ref, v_ref, r_ref, s0_ref, gn_ref, ltri_ref, ones_ref, y_ref,
                sout_ref, st_ref, *, chunk, sub, heads):
    t = pl.program_id(1)
    nb, tb, kw = q_ref.shape
    gw = v_ref.shape[2]
    dk, dv = kw // heads, gw // heads
    n_sub = chunk // sub
    n_chunks = tb // chunk
    group = GLA_GROUP if n_chunks % GLA_GROUP == 0 else 1

    @pl.when(t == 0)
    def _init():
        for bi in range(nb):
            for h in range(heads):
                blocks = [s0_ref[bi, h] if g == h else jnp.zeros((dk, dv), F32)
                          for g in range(heads)]
                st_ref[bi, h * dk:(h + 1) * dk, :] = jnp.concatenate(blocks, axis=1)

    own_state = (lax.broadcasted_iota(jnp.int32, (kw, gw), 0) // dk
                 == lax.broadcasted_iota(jnp.int32, (kw, gw), 1) // dv)
    ones = ones_ref[...]
    lane_head = lax.broadcasted_iota(jnp.int32, (sub, kw), 1) // dk
    srow = lax.broadcasted_iota(jnp.int32, (chunk, kw), 0)
    att_col = lax.broadcasted_iota(jnp.int32, (heads * sub, chunk), 1)
    att_t = lax.broadcasted_iota(jnp.int32, (heads * sub, chunk), 0) % sub
    ltri = ltri_ref[...]
    gn = gn_ref[...]

    def state_free(rows_list):
        n = range(len(rows_list))
        q = [q_ref[bi, r, :] for bi, r in rows_list]
        k = [k_ref[bi, r, :] for bi, r in rows_list]
        v = [v_ref[bi, r, :] for bi, r in rows_list]
        lg = [lg_ref[bi, r, :] for bi, r in rows_list]
        lg_hi = [x.astype(BF16) for x in lg]
        lg_lo = [(x - h.astype(F32)).astype(BF16) for x, h in zip(lg, lg_hi)]
        bc = [_dot(ltri, h) + _dot(ltri, l) for h, l in zip(lg_hi, lg_lo)]
        b_last = [x[chunk - 1:chunk, :] for x in bc]
        k_last = [(k[j] * jnp.exp(b_last[j] - bc[j])).astype(BF16) for j in n]
        update = [jnp.where(own_state, _dot_tn(k_last[j], v[j]), 0.0) for j in n]
        total = [_dot_tn(lg_hi[j], ones) + _dot_tn(lg_lo[j], ones) for j in n]
        decay = [jnp.concatenate([jnp.exp(x)] * (gw // LANES), axis=1) for x in total]
        q_decayed = [(q[j] * jnp.exp(bc[j])).astype(BF16) for j in n]
        intra = [[] for _ in n]
        for sc in range(n_sub):
            lo, hi = sc * sub, (sc + 1) * sub
            seen = srow < hi
            att = []
            for j in n:
                ref = bc[j][lo - 1:lo, :] if sc > 0 else jnp.zeros((1, kw), F32)
                qd = q[j][lo:hi] * jnp.exp(bc[j][lo:hi] - ref)
                kd = (jnp.where(seen, k[j], 0.0)
                      * jnp.exp(jnp.where(seen, ref - bc[j], 0.0))).astype(BF16)
                qs = jnp.concatenate([jnp.where(lane_head == h, qd, 0.0) for h in range(heads)],
                                     axis=0).astype(BF16)
                att.append(_dot_nt(qs, kd))
            for j in n:
                a = jnp.where(att_col <= lo + att_t, att[j], 0.0).astype(BF16)
                ov = _dot(a, v[j])
                intra[j].append(jnp.concatenate(
                    [ov[h * sub:(h + 1) * sub, h * dv:(h + 1) * dv] for h in range(heads)],
                    axis=1))
        return [(q_decayed[j], jnp.concatenate(intra[j], axis=0), decay[j], update[j]) for j in n]

    def finish(rows_list, outs):
        heads_of = [[o[:, h * dv:(h + 1) * dv] for h in range(heads)] for o in outs]
        scale = [[lax.rsqrt(jnp.mean(oh * oh, axis=-1, keepdims=True) + RMS_EPS) for oh in ohs]
                 for ohs in heads_of]
        for (bi, rows), ohs, scs in zip(rows_list, heads_of, scale):
            r = r_ref[bi, rows, :]
            normed = jnp.concatenate([oh * sc for oh, sc in zip(ohs, scs)], axis=1)
            y_ref[bi, rows, :] = (normed * gn * (r * jax.nn.sigmoid(r))).astype(BF16)

    def group_body(gi, carry):
        rows = [(bi, pl.ds(pl.multiple_of((gi * group + j) * chunk, chunk), chunk))
                for j in range(group) for bi in range(nb)]
        parts = state_free(rows)
        st = [st_ref[bi] for bi in range(nb)]
        outs = []
        for (bi, _), (q_decayed, intra, decay, update) in zip(rows, parts):
            outs.append(_dot(q_decayed, st[bi].astype(BF16)) + intra)
            st[bi] = st[bi] * decay + update
        for bi in range(nb):
            st_ref[bi] = st[bi]
        finish(rows, outs)
        return carry

    lax.fori_loop(0, tb // (chunk * group), group_body, 0)

    @pl.when(t == pl.num_programs(1) - 1)
    def _final():
        for bi in range(nb):
            for h in range(heads):
                sout_ref[bi, h] = st_ref[bi, h * dk:(h + 1) * dk, h * dv:(h + 1) * dv]


def _gla(q, k, lg, v, r, s0, g_norm, heads, chunk, tb, nb=1):
    b, t, kw = q.shape
    gw = v.shape[2]
    dk, dv = kw // heads, gw // heads
    sub = min(GLA_SUB, chunk)
    tb = min(tb, t)
    assert t % tb == 0 and tb % chunk == 0 and chunk % sub == 0 and b % nb == 0
    shared_state = s0.shape[0] == 1
    assert not (shared_state and nb > 1)
    ltri = (lax.broadcasted_iota(jnp.int32, (chunk, chunk), 0)
            >= lax.broadcasted_iota(jnp.int32, (chunk, chunk), 1)).astype(BF16)
    gn = jnp.tile(g_norm.astype(F32), heads).reshape(1, gw)
    tok = lambda n: pl.BlockSpec((nb, tb, n), lambda bi, ti: (bi, ti, 0))
    state_in = pl.BlockSpec((nb, heads, dk, dv),
                            lambda bi, ti: (0 if shared_state else bi, 0, 0, 0))
    state_out = pl.BlockSpec((nb, heads, dk, dv), lambda bi, ti: (bi, 0, 0, 0))
    ones = jnp.ones((chunk, LANES), BF16)
    const = lambda shape: pl.BlockSpec(shape, lambda bi, ti: (0,) * len(shape))
    return pl.pallas_call(
        functools.partial(_gla_kernel, chunk=chunk, sub=sub, heads=heads),
        grid=(b // nb, t // tb),
        in_specs=[tok(kw), tok(kw), tok(kw), tok(gw), tok(gw), state_in, const((1, gw)),
                  const((chunk, chunk)), const((chunk, LANES))],
        out_specs=[tok(gw), state_out],
        out_shape=[jax.ShapeDtypeStruct((b, t, gw), BF16),
                   jax.ShapeDtypeStruct((b, heads, dk, dv), F32)],
        scratch_shapes=[pltpu.VMEM((nb, kw, gw), F32)],
        compiler_params=_params(2),
        name="gla",
    )(q, k, lg, v, r, s0, gn, ltri, ones)


def kernel(x_prompt, x_sample, cache_k, cache_v, state_gla, page_table, meta_tokens, ln_g, ln_b,
           w_ffn1_in, w_ffn1_out, w_mix_in, w_gate_up, b_gate, b_sb, g_gla_norm, w_out,
           w_ffn2_in, w_ffn2_out):
    depth = w_out.shape[0]
    assert depth == 1
    alpha = (2.0 * depth) ** 0.25
    b, t, d = x_prompt.shape
    db, ts, _ = x_sample.shape
    _, n_phys, page, heads, dh = cache_k.shape
    w = heads * dh
    _, _, gheads, dk, dv = state_gla.shape
    kw, gw = gheads * dk, gheads * dv
    n_meta = meta_tokens.shape[0]
    lyr = 0

    ffn1 = _prep_ffn(w_ffn1_in[lyr], w_ffn1_out[lyr])
    ffn2 = _prep_ffn(w_ffn2_in[lyr], w_ffn2_out[lyr])
    mix_w = _prep_mix(w_mix_in[lyr], w_gate_up[lyr], b_gate[lyr], w, kw, gw)
    wo_sb = w_out[lyr][:w].astype(BF16)
    wo_gla = w_out[lyr][w:].astype(BF16)
    g, bb = ln_g[lyr], ln_b[lyr]
    bias = b_sb[lyr].astype(F32)
    mix = functools.partial(_mix, mix_w=mix_w, w=w, kw=kw, gw=gw, sb_scale=dh ** -0.5 * LOG2E,
                            gla_scale=dk ** -0.5)

    hm = _ffn_ln(meta_tokens.astype(x_prompt.dtype), ffn1, g[0], bb[0], alpha)
    hp = _ffn_ln(x_prompt.reshape(b * t, d), ffn1, g[0], bb[0], alpha)
    hs = _ffn_ln(x_sample.reshape(db * ts, d), ffn1, g[0], bb[0], alpha)
    _, kmf, vmf, kmb, vmb, gqm, gkm, gvm, grm, lgm = mix(hm)
    qp, kpt, vpt, kpb, vpb, gqp, gkp, gvp, grp, lgp = mix(hp, seqs=b)
    qs, ksf, vsf, _, _, gqs, gks, gvs, grs, lgs = mix(hs)

    r3 = lambda a, n: a.reshape(n, a.shape[0] // n, a.shape[1])
    sb_p = _sb_prompt(bias, r3(qp, b), r3(kpb, b), r3(vpb, b), kmb, vmb, dh)
    zero_state = jnp.zeros((1, gheads, dk, dv), F32)
    _, st_meta = _gla(r3(gqm, 1), r3(gkm, 1), r3(lgm, 1), r3(gvm, 1), r3(grm, 1), zero_state,
                      g_gla_norm[lyr], gheads, n_meta, n_meta)
    gla_p, st_p = _gla(r3(gqp, b), r3(gkp, b), r3(lgp, b), r3(gvp, b), r3(grp, b), st_meta,
                       g_gla_norm[lyr], gheads, GLA_CHUNK, GLA_GROUP * GLA_CHUNK)
    y_prompt = _outproj_ffn_ln(hp, sb_p.reshape(b * t, w), gla_p.reshape(b * t, gw), wo_sb,
                               wo_gla, g[1], bb[1], ffn2, g[2], bb[2], alpha).reshape(b, t, d)

    feature_major = lambda c: c.transpose(0, 2, 3, 1).reshape(n_phys, w, page)
    sb_s = _sb_sample(bias, r3(qs, db).astype(F32), r3(ksf, db), r3(vsf, db),
                      feature_major(cache_k[lyr]), feature_major(cache_v[lyr]), page_table, heads)
    ts_pad = -(-ts // GLA_SUB) * GLA_SUB
    padt = lambda a: jnp.pad(r3(a, db), ((0, 0), (0, ts_pad - ts), (0, 0)))
    gla_s, st_s = _gla(padt(gqs), padt(gks), padt(lgs), padt(gvs), padt(grs),
                       state_gla[lyr].astype(F32), g_gla_norm[lyr], gheads, ts_pad, ts_pad,
                       nb=GLA_SEQS_PER_STEP if db % GLA_SEQS_PER_STEP == 0 else 1)
    y_sample = _outproj_ffn_ln(hs, sb_s.reshape(db * ts, w).astype(BF16),
                               gla_s[:, :ts].reshape(db * ts, gw), wo_sb, wo_gla,
                               g[1], bb[1], ffn2, g[2], bb[2], alpha).reshape(db, ts, d)

    def with_meta(meta_rows, feature_major_rows):
        full = jnp.concatenate(
            [jnp.broadcast_to(meta_rows.T[None], (b, w, n_meta)), feature_major_rows], axis=2)
        return full.reshape(1, b, heads, dh, n_meta + t).transpose(0, 1, 4, 2, 3)

    sdt = state_gla.dtype
    return (y_prompt, y_sample, with_meta(kmf, kpt), with_meta(vmf, vpt),
            st_p.astype(sdt)[None],
            ksf.reshape(1, db, ts, heads, dh), vsf.reshape(1, db, ts, heads, dh),
            st_s.astype(sdt)[None])
```
